```python
import math
import jax, jax.numpy as jnp
from jax import lax
import numpy as np

D_MODEL = 1024
BATCH = 4
SEQ = 4096
DEPTH = 4

HEAD_DIM = 64
ROPE_THETA = 10000.0
BLOCK_Q = 128
NEG_INF = -1e30
NORM_EPS = 1e-6

DA_HEADS = 4
DA_QK = DA_HEADS * 2 * HEAD_DIM
DA_V = DA_HEADS * 2 * HEAD_DIM

GRID_W = 64
NA_HEADS = 4
NA_WIDTH = NA_HEADS * HEAD_DIM
NA_WIN_ROWS = 8
NA_WIN_COLS = 16

DIL_CONFIGS = ((128, 1), (512, 4), (2048, 16))
DIL_GROUPS = 3
DIL_HEADS = 4
DIL_WIDTH = DIL_GROUPS * DIL_HEADS * HEAD_DIM
DIL_OUT = DIL_HEADS * HEAD_DIM

N_BRANCHES = 3
IN_SIZES = (DA_QK, DA_QK, DA_V, NA_WIDTH, NA_WIDTH, NA_WIDTH,
            DIL_WIDTH, DIL_WIDTH, DIL_WIDTH, N_BRANCHES * D_MODEL)
IN_WIDTH = 2 * DA_QK + DA_V + 3 * NA_WIDTH + 3 * DIL_WIDTH + N_BRANCHES * D_MODEL

N_GROUPS = 4
EXPERTS_PER_GROUP = 8
N_EXPERTS = N_GROUPS * EXPERTS_PER_GROUP
TOP_K_FINE = 2
EXPERT_HIDDEN = 512
MOE_BLOCK = 128

kernel_name = 'hybrid_diff_na_dilated_hmoe_adaln'


def rms_norm(x, g):
    xf = x.astype(jnp.float32)
    y = xf * lax.rsqrt(jnp.mean(xf * xf, axis=-1, keepdims=True) + NORM_EPS)
    return (y * g.astype(jnp.float32)).astype(x.dtype)


def rotary_tables(seq, dim, dtype):
    inv = 1.0 / (ROPE_THETA ** (jnp.arange(0, dim, 2, dtype=jnp.float32) / dim))
    ang = jnp.arange(seq, dtype=jnp.float32)[:, None] * inv[None, :]
    ang = jnp.concatenate([ang, ang], axis=-1)
    return jnp.cos(ang).astype(dtype), jnp.sin(ang).astype(dtype)


def apply_rotary(x, cos, sin):
    x1, x2 = jnp.split(x, 2, axis=-1)
    return x * cos + jnp.concatenate([-x2, x1], axis=-1) * sin


def diff_attention(q, k, v, lam_params, subln_g, lam_init, cos, sin):
    B, S, _ = q.shape
    H, dh = DA_HEADS, HEAD_DIM
    q = apply_rotary(q.reshape(B, S, H, 2, dh).transpose(0, 2, 3, 1, 4), cos, sin)
    k = apply_rotary(k.reshape(B, S, H, 2, dh).transpose(0, 2, 3, 1, 4), cos, sin)
    v = v.reshape(B, S, H, 2 * dh).transpose(0, 2, 1, 3)
    lp = lam_params.astype(jnp.float32)
    lam = jnp.exp(jnp.sum(lp[0] * lp[1])) - jnp.exp(jnp.sum(lp[2] * lp[3])) + lam_init
    nblk = S // BLOCK_Q
    qblocks = q.reshape(B, H, 2, nblk, BLOCK_Q, dh).transpose(3, 0, 1, 2, 4, 5)
    scale = HEAD_DIM ** -0.5

    def block(qi):
        s = jnp.einsum('bhmqd,bhmkd->bhmqk', qi, k).astype(jnp.float32) * scale
        p = jax.nn.softmax(s, axis=-1)
        a = p[:, :, 0] - lam * p[:, :, 1]
        return jnp.einsum('bhqk,bhkd->bhqd', a.astype(v.dtype), v)

    o = lax.map(block, qblocks)
    o = o.transpose(1, 0, 3, 2, 4).reshape(B, S, H, 2 * dh)
    o = rms_norm(o, subln_g) * (1.0 - lam_init)
    return o.reshape(B, S, H * 2 * dh)


def neighbourhood_attention(q, k, v, rpb):
    B, S, _ = q.shape
    H, dh = NA_HEADS, HEAD_DIM
    rows = S // GRID_W
    kh = min(NA_WIN_ROWS, rows)
    kw = NA_WIN_COLS
    L = kh * kw
    r = jnp.arange(rows)
    cq = jnp.arange(GRID_W)
    key_rows = jnp.clip(r - kh // 2, 0, rows - kh)[:, None] + jnp.arange(kh)[None, :]
    key_cols = jnp.clip(cq - kw // 2, 0, GRID_W - kw)[:, None] + jnp.arange(kw)[None, :]
    kidx = (key_rows[:, None, :, None] * GRID_W + key_cols[None, :, None, :]).reshape(rows, GRID_W, L)
    br = key_rows - r[:, None] + (NA_WIN_ROWS - 1)
    bc = key_cols - cq[:, None] + (NA_WIN_COLS - 1)
    bias = rpb[:, br[:, None, :, None], bc[None, :, None, :]]
    bias = bias.reshape(H, rows, GRID_W, L).transpose(1, 0, 2, 3).astype(jnp.float32)
    qr = q.reshape(B, rows, GRID_W, H, dh).transpose(1, 0, 3, 2, 4)
    kt = k.reshape(B, S, H, dh).transpose(0, 2, 1, 3)
    vt = v.reshape(B, S, H, dh).transpose(0, 2, 1, 3)
    scale = HEAD_DIM ** -0.5

    def row_block(args):
        qi, idx, bi = args
        kg = kt[:, :, idx]
        vg = vt[:, :, idx]
        s = jnp.einsum('bhqd,bhqld->bhql', qi, kg).astype(jnp.float32) * scale + bi
        p = jax.nn.softmax(s, axis=-1).astype(vg.dtype)
        return jnp.einsum('bhql,bhqld->bhqd', p, vg)

    o = lax.map(row_block, (qr, kidx, bias))
    return o.transpose(1, 0, 3, 2, 4).reshape(B, S, H * dh)


def dilated_group(q, k, v, window, dilation):
    B, H, S, dh = q.shape
    n_side = window // 2 // dilation
    offs = dilation * jnp.arange(-n_side, n_side + 1)
    nblk = S // BLOCK_Q
    qblocks = q.reshape(B, H, nblk, BLOCK_Q, dh).transpose(2, 0, 1, 3, 4)
    tblocks = jnp.arange(S).reshape(nblk, BLOCK_Q)
    scale = HEAD_DIM ** -0.5

    def block(args):
        qi, t = args
        pos = t[:, None] + offs[None, :]
        valid = (pos >= 0) & (pos < S)
        pos = jnp.clip(pos, 0, S - 1)
        kk = k[:, :, pos]
        vv = v[:, :, pos]
        s = jnp.einsum('bhqd,bhqld->bhql', qi, kk).astype(jnp.float32) * scale
        s = jnp.where(valid, s, NEG_INF)
        lse = jax.nn.logsumexp(s, axis=-1)
        p = jnp.exp(s - lse[..., None])
        return jnp.einsum('bhql,bhqld->bhqd', p.astype(vv.dtype), vv), lse

    o, lse = lax.map(block, (qblocks, tblocks))
    o = o.transpose(1, 2, 0, 3, 4).reshape(B, H, S, dh)
    lse = lse.transpose(1, 2, 0, 3).reshape(B, H, S)
    return o, lse


def dilated_attention(q, k, v, cos, sin):
    B, S, _ = q.shape
    G, H, dh = DIL_GROUPS, DIL_HEADS, HEAD_DIM
    q = apply_rotary(q.reshape(B, S, G, H, dh).transpose(2, 0, 3, 1, 4), cos, sin)
    k = apply_rotary(k.reshape(B, S, G, H, dh).transpose(2, 0, 3, 1, 4), cos, sin)
    v = v.reshape(B, S, G, H, dh).transpose(2, 0, 3, 1, 4)
    outs, lses = [], []
    for g, (window, dilation) in enumerate(DIL_CONFIGS):
        o, lse = dilated_group(q[g], k[g], v[g], window, dilation)
        outs.append(o)
        lses.append(lse)
    alpha = jax.nn.softmax(jnp.stack(lses, axis=0), axis=0)
    o = jnp.sum(alpha[..., None].astype(v.dtype) * jnp.stack(outs, axis=0), axis=0)
    return o.transpose(0, 2, 1, 3).reshape(B, S, H * dh)


def hybrid_mixer(h, w_in, da_lam, da_subln_g, lam_init, na_rpb, w_proj_a, w_proj_n, w_proj_d, w_out, cos, sin):
    z = h @ w_in
    splits = np.cumsum(IN_SIZES)[:-1].tolist()
    q_a, k_a, v_a, q_n, k_n, v_n, q_d, k_d, v_d, gates = jnp.split(z, splits, axis=-1)
    o_a = diff_attention(q_a, k_a, v_a, da_lam, da_subln_g, lam_init, cos, sin)
    o_n = neighbourhood_attention(q_n, k_n, v_n, na_rpb)
    o_d = dilated_attention(q_d, k_d, v_d, cos, sin)
    g_a, g_n, g_d = jnp.split(jax.nn.sigmoid(gates), N_BRANCHES, axis=-1)
    merged = g_a * (o_a @ w_proj_a) + g_n * (o_n @ w_proj_n) + g_d * (o_d @ w_proj_d)
    return merged @ w_out


def grouped_swiglu(xt, expert, weight, w1, w3, w2):
    N, D = xt.shape
    A = N * TOP_K_FINE
    flat_e = expert.reshape(-1)
    flat_w = weight.reshape(-1)
    order = jnp.argsort(flat_e)
    sorted_e = flat_e[order]
    tok = order // TOP_K_FINE
    sizes = jnp.bincount(flat_e, length=N_EXPERTS)
    padded = ((sizes + MOE_BLOCK - 1) // MOE_BLOCK) * MOE_BLOCK
    seg_start = jnp.cumsum(sizes) - sizes
    pad_end = jnp.cumsum(padded)
    pad_start = pad_end - padded
    dest = pad_start[sorted_e] + (jnp.arange(A) - seg_start[sorted_e])
    n_blocks = A // MOE_BLOCK + N_EXPERTS
    M = n_blocks * MOE_BLOCK
    slot_tok = jnp.zeros((M,), jnp.int32).at[dest].set(tok.astype(jnp.int32))
    slot_w = jnp.zeros((M,), jnp.float32).at[dest].set(flat_w[order])
    block_e = jnp.searchsorted(pad_end, jnp.arange(n_blocks) * MOE_BLOCK, side='right')
    block_e = jnp.minimum(block_e, N_EXPERTS - 1)
    xs = xt[slot_tok].reshape(n_blocks, MOE_BLOCK, D)

    def run(args):
        xb, e = args
        return (jax.nn.silu(xb @ w1[e]) * (xb @ w3[e])) @ w2[e]

    y = lax.map(run, (xs, block_e)).reshape(M, D)
    y = y * slot_w[:, None].astype(y.dtype)
    return jnp.zeros_like(xt).at[slot_tok].add(y)


def hier_moe(h, wg, bg, we, be, w1, w3, w2):
    B, S, D = h.shape
    xt = h.reshape(B * S, D)
    N = xt.shape[0]
    pg = jax.nn.softmax((xt @ wg + bg).astype(jnp.float32), axis=-1)
    pg_top, g_idx = lax.top_k(pg, 1)
    le = (xt @ we + be).astype(jnp.float32).reshape(N, N_GROUPS, EXPERTS_PER_GROUP)
    le = jnp.take_along_axis(le, g_idx[:, :, None], axis=1)[:, 0]
    pe_top, e_idx = lax.top_k(jax.nn.softmax(le, axis=-1), TOP_K_FINE)
    pe_top = pe_top / jnp.sum(pe_top, axis=-1, keepdims=True)
    weight = pg_top * pe_top
    expert = g_idx * EXPERTS_PER_GROUP + e_idx
    return grouped_swiglu(xt, expert, weight, w1, w3, w2).reshape(B, S, D)


def setup_inputs(seed: int = 0) -> dict:
    key = jax.random.key(seed)
    ks = jax.random.split(key, 24)
    f32 = jnp.float32
    D = D_MODEL

    def nrm(k, shape, scale):
        return jax.random.normal(k, shape, f32) * scale

    return {
        'x': nrm(ks[0], (BATCH, SEQ, D), 1.0),
        'c': nrm(ks[1], (BATCH, D), 1.0),
        'ada_w': nrm(ks[2], (DEPTH, D, 6 * D), 0.5 * D ** -0.5),
        'ada_b': nrm(ks[3], (DEPTH, 6 * D), 0.02),
        'mix_norm_g': 1.0 + nrm(ks[4], (DEPTH, D), 0.02),
        'ffn_norm_g': 1.0 + nrm(ks[5], (DEPTH, D), 0.02),
        'w_in': nrm(ks[6], (DEPTH, D, IN_WIDTH), D ** -0.5),
        'da_lambda': nrm(ks[7], (DEPTH, 4, HEAD_DIM), 0.1),
        'da_subln_g': 1.0 + nrm(ks[8], (DEPTH, 2 * HEAD_DIM), 0.02),
        'na_rpb': nrm(ks[9], (DEPTH, NA_HEADS, 2 * NA_WIN_ROWS - 1, 2 * NA_WIN_COLS - 1), 0.02),
        'w_proj_a': nrm(ks[10], (DEPTH, DA_V, D), DA_V ** -0.5),
        'w_proj_n': nrm(ks[11], (DEPTH, NA_WIDTH, D), NA_WIDTH ** -0.5),
        'w_proj_d': nrm(ks[12], (DEPTH, DIL_OUT, D), DIL_OUT ** -0.5),
        'w_out': nrm(ks[13], (DEPTH, D, D), D ** -0.5),
        'router_group_w': nrm(ks[14], (DEPTH, D, N_GROUPS), D ** -0.5),
        'router_group_b': nrm(ks[15], (DEPTH, N_GROUPS), 0.01),
        'router_expert_w': nrm(ks[16], (DEPTH, D, N_EXPERTS), D ** -0.5),
        'router_expert_b': nrm(ks[17], (DEPTH, N_EXPERTS), 0.01),
        'expert_w1': nrm(ks[18], (DEPTH, N_EXPERTS, D, EXPERT_HIDDEN), D ** -0.5),
        'expert_w3': nrm(ks[19], (DEPTH, N_EXPERTS, D, EXPERT_HIDDEN), D ** -0.5),
        'expert_w2': nrm(ks[20], (DEPTH, N_EXPERTS, EXPERT_HIDDEN, D), EXPERT_HIDDEN ** -0.5),
        'final_norm_g': 1.0 + nrm(ks[21], (D,), 0.02),
    }


def reference(x, c, ada_w, ada_b, mix_norm_g, ffn_norm_g, w_in, da_lambda, da_subln_g, na_rpb,
              w_proj_a, w_proj_n, w_proj_d, w_out, router_group_w, router_group_b,
              router_expert_w, router_expert_b, expert_w1, expert_w3, expert_w2, final_norm_g):
    B, S, D = x.shape
    cos, sin = rotary_tables(S, HEAD_DIM, x.dtype)
    c_act = jax.nn.silu(c)
    for l in range(DEPTH):
        mod = c_act @ ada_w[l] + ada_b[l]
        shift_m, scale_m, gate_m, shift_f, scale_f, gate_f = jnp.split(mod[:, None, :], 6, axis=-1)
        lam_init = 0.8 - 0.6 * math.exp(-0.3 * l)
        h = rms_norm(x, mix_norm_g[l]) * (1.0 + scale_m) + shift_m
        y = hybrid_mixer(h, w_in[l], da_lambda[l], da_subln_g[l], lam_init, na_rpb[l],
                         w_proj_a[l], w_proj_n[l], w_proj_d[l], w_out[l], cos, sin)
        x = x + gate_m * y
        h = rms_norm(x, ffn_norm_g[l]) * (1.0 + scale_f) + shift_f
        y = hier_moe(h, router_group_w[l], router_group_b[l], router_expert_w[l], router_expert_b[l],
                     expert_w1[l], expert_w3[l], expert_w2[l])
        x = x + gate_f * y
    return rms_norm(x, final_norm_g)
```

```python
import functools
import math

import jax
import jax.numpy as jnp
from jax import lax
from jax.experimental import pallas as pl
from jax.experimental.pallas import tpu as pltpu

F32 = jnp.float32
BF16 = jnp.bfloat16
I32 = jnp.int32
HIGHEST = lax.Precision.HIGHEST

HEAD_DIM = 64
ROPE_THETA = 10000.0
NORM_EPS = 1e-6
NEG_INF = -1e30

DA_HEADS = 4
GRID_W = 64
NA_HEADS = 4
NA_WIN_ROWS = 8
NA_WIN_COLS = 16
DIL_DILATIONS = (1, 4, 16)
DIL_SIDE = 64
DIL_HEADS = 4
N_GROUPS = 4
EXPERTS_PER_GROUP = 8
N_EXPERTS = N_GROUPS * EXPERTS_PER_GROUP

LANES = 128
SUBLANES = 8
MOE_BLK = 256
VMEM_LIMIT = 56 * 1024 * 1024

C_QA, C_KA, C_QD, C_KD = 0, 512, 1024, 1792
N_ROT = 2560
C_VA, C_GATE, C_QN, C_KN, C_VN, C_VD = 2560, 3072, 6144, 6400, 6656, 6912
NC = 7680
IN_PERM = ((0, 1024), (2304, 3840), (1024, 1536), (4608, 7680), (1536, 2304), (3840, 4608))


def _nt_dot(a, b):
    return lax.dot_general(a, b, (((1,), (1,)), ((), ())), preferred_element_type=F32)


def _params(*sem):
    return pltpu.CompilerParams(dimension_semantics=sem, vmem_limit_bytes=VMEM_LIMIT)


def _ada_kernel(c_ref, w_ref, b_ref, o_ref):
    c = c_ref[...]
    ca = c / (1.0 + jnp.exp(-c))
    o_ref[...] = jnp.dot(ca, w_ref[...], preferred_element_type=F32, precision=HIGHEST) + b_ref[...]


def _ada(c, ada_w, ada_b):
    depth, d, w6 = ada_w.shape
    bsz = c.shape[0]
    rows = -(-bsz // SUBLANES) * SUBLANES
    cp = jnp.zeros((rows, d), F32).at[:bsz].set(c)
    tn = w6 // 4
    out = pl.pallas_call(
        _ada_kernel,
        grid=(depth, w6 // tn),
        in_specs=[pl.BlockSpec((rows, d), lambda l, j: (0, 0)),
                  pl.BlockSpec((None, d, tn), lambda l, j: (l, 0, j)),
                  pl.BlockSpec((None, 1, tn), lambda l, j: (l, 0, j))],
        out_specs=pl.BlockSpec((None, rows, tn), lambda l, j: (l, 0, j)),
        out_shape=jax.ShapeDtypeStruct((depth, rows, w6), F32),
        compiler_params=_params("arbitrary", "arbitrary"),
        name="ada_mod",
    )(cp, ada_w, ada_b.reshape(depth, 1, w6))
    return out[:, :bsz]


def _inproj_kernel(x_ref, g_ref, sc_ref, sh_ref, cos_ref, sin_ref, cs_ref, w_ref, z_ref, h_scr, *, n_rot_tiles):
    j = pl.program_id(1)

    @pl.when(j == 0)
    def _():
        x = x_ref[...]
        ms = jnp.mean(x * x, axis=-1, keepdims=True)
        y = x * lax.rsqrt(ms + NORM_EPS) * g_ref[...]
        h_scr[...] = (y * (1.0 + sc_ref[...]) + sh_ref[...]).astype(BF16)

    acc = jnp.dot(h_scr[...], w_ref[...], preferred_element_type=F32) * cs_ref[...]

    @pl.when(j < n_rot_tiles)
    def _():
        half = HEAD_DIM // 2
        lane = lax.broadcasted_iota(I32, (acc.shape[0], LANES), 1)
        lo = (lane % HEAD_DIM) < half
        cos = cos_ref[...]
        sin = sin_ref[...]
        for cc in range(acc.shape[1] // LANES):
            a = acc[:, cc * LANES:(cc + 1) * LANES]
            rot = jnp.where(lo, pltpu.roll(a, LANES - half, 1), pltpu.roll(a, half, 1))
            z_ref[:, cc * LANES:(cc + 1) * LANES] = (a * cos + rot * sin).astype(BF16)

    @pl.when(j >= n_rot_tiles)
    def _():
        z_ref[...] = acc.astype(BF16)


def _inproj(x2, g, scale, shift, cos_t, sin_t, colscale, w_bf, seq, tm=512, tn=512):
    n, d = x2.shape
    nc = w_bf.shape[1]
    spb = seq // tm
    return pl.pallas_call(
        functools.partial(_inproj_kernel, n_rot_tiles=N_ROT // tn),
        grid=(n // tm, nc // tn),
        in_specs=[pl.BlockSpec((tm, d), lambda i, j: (i, 0)),
                  pl.BlockSpec((1, d), lambda i, j: (0, 0)),
                  pl.BlockSpec((None, 1, d), lambda i, j: (i // spb, 0, 0)),
                  pl.BlockSpec((None, 1, d), lambda i, j: (i // spb, 0, 0)),
                  pl.BlockSpec((tm, LANES), lambda i, j: (i % spb, 0)),
                  pl.BlockSpec((tm, LANES), lambda i, j: (i % spb, 0)),
                  pl.BlockSpec((1, tn), lambda i, j: (0, j)),
                  pl.BlockSpec((d, tn), lambda i, j: (0, j))],
        out_specs=pl.BlockSpec((tm, tn), lambda i, j: (i, j)),
        out_shape=jax.ShapeDtypeStruct((n, nc), BF16),
        scratch_shapes=[pltpu.VMEM((tm, d), BF16)],
        compiler_params=_params("arbitrary", "arbitrary"),
        name="inproj",
    )(x2, g, scale, shift, cos_t, sin_t, colscale, w_bf)


def _diff_attn_kernel(lam_ref, g_ref, q_ref, k_ref, v_ref, o_ref, *, lam_init):
    lp = lam_ref[...]
    lam = (jnp.exp(jnp.sum(lp[0:1] * lp[1:2], axis=-1, keepdims=True))
           - jnp.exp(jnp.sum(lp[2:3] * lp[3:4], axis=-1, keepdims=True)) + lam_init)
    q = q_ref[...]
    k = k_ref[...]
    v = v_ref[...]
    lane = lax.broadcasted_iota(I32, q.shape, 1)
    outs = []
    for m in range(2):
        qm = jnp.where((lane >= HEAD_DIM * m) & (lane < HEAD_DIM * (m + 1)), q, jnp.zeros_like(q))
        s = _nt_dot(qm, k)
        mx = jnp.max(s, axis=-1, keepdims=True)
        p = jnp.exp(s - mx)
        l = jnp.sum(p, axis=-1, keepdims=True)
        outs.append(jnp.dot(p.astype(BF16), v, preferred_element_type=F32) / l)
    o = outs[0] - lam * outs[1]
    ms = jnp.mean(o * o, axis=-1, keepdims=True)
    o = o * lax.rsqrt(ms + NORM_EPS) * g_ref[...] * (1.0 - lam_init)
    o_ref[...] = o.astype(BF16)


def _diff_attn(z, lam_p, subln_g, lam_init, bsz, seq, tq=256):
    n = z.shape[0]
    hw = 2 * HEAD_DIM
    nq = seq // tq
    return pl.pallas_call(
        functools.partial(_diff_attn_kernel, lam_init=lam_init),
        grid=(bsz, DA_HEADS, nq),
        in_specs=[pl.BlockSpec((4, HEAD_DIM), lambda b, h, i: (0, 0)),
                  pl.BlockSpec((1, hw), lambda b, h, i: (0, 0)),
                  pl.BlockSpec((tq, hw), lambda b, h, i: (b * nq + i, C_QA // hw + h)),
                  pl.BlockSpec((seq, hw), lambda b, h, i: (b, C_KA // hw + h)),
                  pl.BlockSpec((seq, hw), lambda b, h, i: (b, C_VA // hw + h))],
        out_specs=pl.BlockSpec((tq, hw), lambda b, h, i: (b * nq + i, h)),
        out_shape=jax.ShapeDtypeStruct((n, DA_HEADS * hw), BF16),
        compiler_params=_params("arbitrary", "arbitrary", "arbitrary"),
        name="diff_attn",
    )(lam_p, subln_g, z, z, z)


def _na_kernel(q_ref, k_ref, v_ref, bias_ref, o_ref, *, rows):
    width = NA_HEADS * HEAD_DIM
    win = NA_WIN_ROWS * GRID_W
    lane = lax.broadcasted_iota(I32, (GRID_W, width), 1)

    def body(r, carry):
        kr0 = jnp.clip(r - NA_WIN_ROWS // 2, 0, rows - NA_WIN_ROWS)
        pat = kr0 - r + (NA_WIN_ROWS - 1)
        q = q_ref[pl.ds(pl.multiple_of(r * GRID_W, GRID_W), GRID_W), :]
        kw = k_ref[pl.ds(pl.multiple_of(kr0 * GRID_W, GRID_W), win), :]
        vw = v_ref[pl.ds(pl.multiple_of(kr0 * GRID_W, GRID_W), win), :]
        o = jnp.zeros((GRID_W, width), F32)
        for h in range(NA_HEADS):
            hm = (lane >= HEAD_DIM * h) & (lane < HEAD_DIM * (h + 1))
            qh = jnp.where(hm, q, jnp.zeros_like(q))
            s = _nt_dot(qh, kw) + bias_ref[pat, h]
            mx = jnp.max(s, axis=-1, keepdims=True)
            p = jnp.exp(s - mx)
            l = jnp.sum(p, axis=-1, keepdims=True)
            oh = jnp.dot(p.astype(BF16), vw, preferred_element_type=F32) / l
            o = jnp.where(hm, oh, o)
        o_ref[pl.ds(pl.multiple_of(r * GRID_W, GRID_W), GRID_W), :] = o.astype(BF16)
        return carry

    lax.fori_loop(0, rows, body, 0)


def _na_bias_table(rpb):
    pat = jnp.arange(NA_WIN_ROWS)[:, None]
    krl = jnp.arange(NA_WIN_ROWS)[None, :]
    ri = pat + krl
    c = jnp.arange(GRID_W)[:, None]
    kc = jnp.arange(GRID_W)[None, :]
    kc0 = jnp.clip(c - NA_WIN_COLS // 2, 0, GRID_W - NA_WIN_COLS)
    valid = (kc >= kc0) & (kc < kc0 + NA_WIN_COLS)
    ci = jnp.clip(kc - c + (NA_WIN_COLS - 1), 0, 2 * NA_WIN_COLS - 2)
    tab = rpb[:, ri[:, :, None, None], ci[None, None, :, :]]
    tab = jnp.where(valid[None, None, None], tab.astype(F32), NEG_INF)
    tab = tab.transpose(1, 0, 3, 2, 4)
    return tab.reshape(NA_WIN_ROWS, NA_HEADS, GRID_W, NA_WIN_ROWS * GRID_W)


def _na_attn(z, bias_tab, bsz, seq):
    n = z.shape[0]
    width = NA_HEADS * HEAD_DIM
    rows = seq // GRID_W
    assert rows >= NA_WIN_ROWS
    return pl.pallas_call(
        functools.partial(_na_kernel, rows=rows),
        grid=(bsz,),
        in_specs=[pl.BlockSpec((seq, width), lambda b: (b, C_QN // width)),
                  pl.BlockSpec((seq, width), lambda b: (b, C_KN // width)),
                  pl.BlockSpec((seq, width), lambda b: (b, C_VN // width)),
                  pl.BlockSpec(bias_tab.shape, lambda b: (0, 0, 0, 0))],
        out_specs=pl.BlockSpec((seq, width), lambda b: (b, 0)),
        out_shape=jax.ShapeDtypeStruct((n, width), BF16),
        compiler_params=_params("arbitrary"),
        name="na_attn",
    )(z, z, z, bias_tab)


def _dil_kernel(q_ref, k_ref, v_ref, o_ref, lse_ref, *scratch, dil, seg, win, tq):
    width = DIL_HEADS * HEAD_DIM
    nlc = width // LANES
    qi = pl.program_id(1)
    if dil > 1:
        kd_scr, vd_scr, stage_scr, ostage_scr, lstage_scr = scratch

        @pl.when(qi == 0)
        def _():
            for src, dst in ((k_ref, kd_scr), (v_ref, vd_scr)):
                for lc in range(nlc):
                    stage_scr[lc] = src[:, lc * LANES:(lc + 1) * LANES].astype(F32)
                for r in range(dil):
                    for lc in range(nlc):
                        dst[r * seg:(r + 1) * seg, lc * LANES:(lc + 1) * LANES] = (
                            stage_scr[lc, pl.ds(r, seg, stride=dil), :].astype(BF16))

        for lc in range(nlc):
            stage_scr[lc, 0:tq * dil, :] = q_ref[:, lc * LANES:(lc + 1) * LANES].astype(F32)
        k_src, v_src = kd_scr, vd_scr
    else:
        k_src, v_src = k_ref, v_ref
    a0 = qi * tq
    w0 = jnp.clip(a0 - DIL_SIDE, 0, seg - win)
    aq = a0 + lax.broadcasted_iota(I32, (tq, win), 0)
    ak = w0 + lax.broadcasted_iota(I32, (tq, win), 1)
    valid = jnp.abs(aq - ak) <= DIL_SIDE
    lane = lax.broadcasted_iota(I32, (tq, width), 1)

    def residue(r):
        if dil > 1:
            q = jnp.concatenate([stage_scr[lc, pl.ds(r, tq, stride=dil), :] for lc in range(nlc)], axis=1).astype(BF16)
        else:
            q = q_ref[...]
        start = pl.multiple_of(r * seg + w0, DIL_SIDE)
        kw = k_src[pl.ds(start, win), :]
        vw = v_src[pl.ds(start, win), :]
        o = jnp.zeros((tq, width), F32)
        lse = jnp.zeros((tq, width), F32)
        for h in range(DIL_HEADS):
            hm = (lane >= HEAD_DIM * h) & (lane < HEAD_DIM * (h + 1))
            qh = jnp.where(hm, q, jnp.zeros_like(q))
            s = jnp.where(valid, _nt_dot(qh, kw), NEG_INF)
            mx = jnp.max(s, axis=-1, keepdims=True)
            p = jnp.exp(s - mx)
            l = jnp.sum(p, axis=-1, keepdims=True)
            oh = jnp.dot(p.astype(BF16), vw, preferred_element_type=F32) / l
            o = jnp.where(hm, oh, o)
            lse = jnp.where(hm, mx + jnp.log(l), lse)
        if dil > 1:
            for lc in range(nlc):
                ostage_scr[lc, pl.ds(r, tq, stride=dil), :] = o[:, lc * LANES:(lc + 1) * LANES]
                lstage_scr[lc, pl.ds(r, tq, stride=dil), :] = lse[:, lc * LANES:(lc + 1) * LANES]
        else:
            o_ref[...] = o
            lse_ref[...] = lse

    if dil == 1:
        residue(0)
    else:
        def body(r, carry):
            residue(r)
            return carry

        lax.fori_loop(0, dil, body, 0)
        for lc in range(nlc):
            o_ref[:, lc * LANES:(lc + 1) * LANES] = ostage_scr[lc]
            lse_ref[:, lc * LANES:(lc + 1) * LANES] = lstage_scr[lc]


def _dil_attn(z, g, bsz, seq):
    dil = DIL_DILATIONS[g]
    n = z.shape[0]
    width = DIL_HEADS * HEAD_DIM
    seg = seq // dil
    tq = min(128, seg)
    win = min(2 * DIL_SIDE + tq, seg)
    nq = seg // tq
    cq, ck, cv = (C_QD // width + g, C_KD // width + g, C_VD // width + g)
    scratch = []
    if dil > 1:
        nlc = width // LANES
        scratch = [pltpu.VMEM((seq, width), BF16), pltpu.VMEM((seq, width), BF16),
                   pltpu.VMEM((nlc, seq, LANES), F32),
                   pltpu.VMEM((nlc, tq * dil, LANES), F32), pltpu.VMEM((nlc, tq * dil, LANES), F32)]
    return pl.pallas_call(
        functools.partial(_dil_kernel, dil=dil, seg=seg, win=win, tq=tq),
        grid=(bsz, nq),
        in_specs=[pl.BlockSpec((tq * dil, width), lambda b, i: (b * nq + i, cq)),
                  pl.BlockSpec((seq, width), lambda b, i: (b, ck)),
                  pl.BlockSpec((seq, width), lambda b, i: (b, cv))],
        out_specs=[pl.BlockSpec((tq * dil, width), lambda b, i: (b * nq + i, 0)),
                   pl.BlockSpec((tq * dil, width), lambda b, i: (b * nq + i, 0))],
        out_shape=[jax.ShapeDtypeStruct((n, width), F32), jax.ShapeDtypeStruct((n, width), F32)],
        scratch_shapes=scratch,
        compiler_params=_params("arbitrary", "arbitrary"),
        name=f"dil_attn_g{g}",
    )(z, z, z)


def _mix_out_kernel(x_ref, oa_ref, on_ref, od0_ref, od1_ref, od2_ref, l0_ref, l1_ref, l2_ref,
                    ga_ref, gn_ref, gd_ref, wpa_ref, wpn_ref, wpd_ref, wout_ref,
                    gm_ref, fg_ref, fsc_ref, fsh_ref, wr_ref, br_ref,
                    xo_ref, h2_ref, lg_ref):
    tm = x_ref.shape[0]
    l0, l1, l2 = l0_ref[...], l1_ref[...], l2_ref[...]
    mx = jnp.maximum(jnp.maximum(l0, l1), l2)
    e0, e1, e2 = jnp.exp(l0 - mx), jnp.exp(l1 - mx), jnp.exp(l2 - mx)
    den = e0 + e1 + e2
    od = (e0 / den) * od0_ref[...] + (e1 / den) * od1_ref[...] + (e2 / den) * od2_ref[...]

    def sig(ref):
        g = ref[...].astype(F32)
        return 1.0 / (1.0 + jnp.exp(-g))

    merged = (sig(ga_ref) * jnp.dot(oa_ref[...], wpa_ref[...], preferred_element_type=F32)
              + sig(gn_ref) * jnp.dot(on_ref[...], wpn_ref[...], preferred_element_type=F32)
              + sig(gd_ref) * jnp.dot(od.astype(BF16), wpd_ref[...], preferred_element_type=F32))
    y = jnp.dot(merged.astype(BF16), wout_ref[...], preferred_element_type=F32)
    xn = x_ref[...] + gm_ref[...] * y
    xo_ref[...] = xn
    ms = jnp.mean(xn * xn, axis=-1, keepdims=True)
    h2 = xn * lax.rsqrt(ms + NORM_EPS) * fg_ref[...]
    h2 = h2 * (1.0 + fsc_ref[...]) + fsh_ref[...]
    lg_ref[...] = jnp.dot(h2, wr_ref[...], preferred_element_type=F32, precision=HIGHEST) + br_ref[...]
    for c in range(h2.shape[1] // LANES):
        h2_ref[pl.ds(c, tm, stride=SUBLANES), :] = h2[:, c * LANES:(c + 1) * LANES]


def _mix_out(x2, z, o_a, o_n, o_d, lse_d, wpa, wpn, wpd, wout, gate_m, fg, fsc, fsh, wr, br, seq, tm=256):
    n, d = x2.shape
    assert d // LANES == SUBLANES and C_GATE % d == 0
    spb = seq // tm
    gcol = C_GATE // d
    wd = DIL_HEADS * HEAD_DIM
    row = lambda w: pl.BlockSpec((tm, w), lambda i: (i, 0))
    full = lambda a: pl.BlockSpec(a.shape, lambda i: (0,) * a.ndim)
    perb = pl.BlockSpec((None, 1, d), lambda i: (i // spb, 0, 0))
    return pl.pallas_call(
        _mix_out_kernel,
        grid=(n // tm,),
        in_specs=[row(d), row(o_a.shape[1]), row(o_n.shape[1]), row(wd), row(wd), row(wd), row(wd), row(wd), row(wd),
                  pl.BlockSpec((tm, d), lambda i: (i, gcol)),
                  pl.BlockSpec((tm, d), lambda i: (i, gcol + 1)),
                  pl.BlockSpec((tm, d), lambda i: (i, gcol + 2)),
                  full(wpa), full(wpn), full(wpd), full(wout),
                  perb, full(fg), perb, perb, full(wr), full(br)],
        out_specs=[row(d), pl.BlockSpec((tm * SUBLANES, LANES), lambda i: (i, 0)), row(LANES)],
        out_shape=[jax.ShapeDtypeStruct((n, d), F32),
                   jax.ShapeDtypeStruct((n * SUBLANES, LANES), F32),
                   jax.ShapeDtypeStruct((n, LANES), F32)],
        compiler_params=_params("arbitrary"),
        name="mix_out",
    )(x2, o_a, o_n, o_d[0], o_d[1], o_d[2], lse_d[0], lse_d[1], lse_d[2], z, z, z,
      wpa, wpn, wpd, wout, gate_m, fg, fsc, fsh, wr, br)


def _route_kernel(lg_ref, e_ref, w_ref, cnt_ref):
    i = pl.program_id(0)
    le = lg_ref[0:N_EXPERTS, :]
    gl = lg_ref[N_EXPERTS:N_EXPERTS + N_GROUPS, :]
    tb = le.shape[1]
    gmax = jnp.max(gl, axis=0, keepdims=True)
    grow = lax.broadcasted_iota(I32, gl.shape, 0)
    gidx = jnp.min(jnp.where(gl == gmax, grow, N_GROUPS), axis=0, keepdims=True)
    pg_top = 1.0 / jnp.sum(jnp.exp(gl - gmax), axis=0, keepdims=True)
    erow = lax.broadcasted_iota(I32, le.shape, 0)
    lm = jnp.where((erow // EXPERTS_PER_GROUP) == gidx, le, -jnp.inf)
    m1 = jnp.max(lm, axis=0, keepdims=True)
    i1 = jnp.min(jnp.where(lm == m1, erow, N_EXPERTS), axis=0, keepdims=True)
    lm2 = jnp.where(erow == i1, -jnp.inf, lm)
    m2 = jnp.max(lm2, axis=0, keepdims=True)
    i2 = jnp.min(jnp.where(lm2 == m2, erow, N_EXPERTS), axis=0, keepdims=True)
    t = jnp.exp(m2 - m1)
    e_ref[0:1, :] = i1
    e_ref[1:2, :] = i2
    w_ref[0:1, :] = pg_top / (1.0 + t)
    w_ref[1:2, :] = pg_top * t / (1.0 + t)
    oh = (erow == i1).astype(F32) + (erow == i2).astype(F32)
    cnt = jnp.sum(oh, axis=1, keepdims=True)

    @pl.when(i == 0)
    def _():
        cnt_ref[...] = jnp.zeros_like(cnt_ref)

    cnt_ref[...] += jnp.broadcast_to(cnt, cnt_ref.shape)


def _route(lgt, tb=1024):
    rows, n = lgt.shape
    tb = min(tb, n)
    return pl.pallas_call(
        _route_kernel,
        grid=(n // tb,),
        in_specs=[pl.BlockSpec((rows, tb), lambda i: (0, i))],
        out_specs=[pl.BlockSpec((2, tb), lambda i: (0, i)),
                   pl.BlockSpec((2, tb), lambda i: (0, i)),
                   pl.BlockSpec((N_EXPERTS, LANES), lambda i: (0, 0))],
        out_shape=[jax.ShapeDtypeStruct((2, n), I32),
                   jax.ShapeDtypeStruct((2, n), F32),
                   jax.ShapeDtypeStruct((N_EXPERTS, LANES), F32)],
        compiler_params=_params("arbitrary"),
        name="route",
    )(lgt)


def _rank_kernel(e_ref, cnt_ref, dest_ref, be_ref, nu_ref, carry_scr, base_scr, *, blk):
    i = pl.program_id(0)

    @pl.when(i == 0)
    def _():
        cnt = cnt_ref[...]
        padded = jnp.floor((cnt + (blk - 1)) / blk) * blk
        r = lax.broadcasted_iota(I32, cnt.shape, 0)
        c = lax.broadcasted_iota(I32, cnt.shape, 1)
        prow = jnp.sum(jnp.where(r == c, padded, 0.0), axis=0, keepdims=True)
        pad_end = jnp.sum(jnp.where(c <= r, prow, 0.0), axis=1, keepdims=True)
        base_scr[...] = jnp.broadcast_to(pad_end, cnt.shape) - padded
        carry_scr[...] = jnp.zeros_like(carry_scr)
        jb = lax.broadcasted_iota(I32, (N_EXPERTS, be_ref.shape[1]), 1).astype(F32) * blk
        be = jnp.sum((pad_end <= jb).astype(I32), axis=0, keepdims=True)
        be_ref[...] = jnp.minimum(be, N_EXPERTS - 1)
        nu_ref[...] = jnp.broadcast_to(jnp.max(pad_end, axis=0, keepdims=True) / blk, nu_ref.shape).astype(I32)

    e = e_ref[...]
    tb = e.shape[1]
    erow = lax.broadcasted_iota(I32, (N_EXPERTS, tb), 0)
    oh0 = erow == e[0:1]
    oh1 = erow == e[1:2]
    both = jnp.where(oh0 | oh1, 1.0, 0.0)
    upper = jnp.where(lax.broadcasted_iota(I32, (tb, tb), 0) < lax.broadcasted_iota(I32, (tb, tb), 1), 1.0, 0.0)
    prefix = jnp.dot(both.astype(BF16), upper.astype(BF16), preferred_element_type=F32)
    tot = prefix + base_scr[:, 0:1] + carry_scr[:, 0:1]
    dest_ref[0:1, :] = jnp.sum(jnp.where(oh0, tot, 0.0), axis=0, keepdims=True).astype(I32)
    dest_ref[1:2, :] = jnp.sum(jnp.where(oh1, tot, 0.0), axis=0, keepdims=True).astype(I32)
    carry_scr[...] += jnp.broadcast_to(jnp.sum(both, axis=1, keepdims=True), carry_scr.shape)


def _rank(eidx, cnt, n_blocks, blk, tb=512):
    n = eidx.shape[1]
    tb = min(tb, n)
    nbp = -(-n_blocks // LANES) * LANES
    return pl.pallas_call(
        functools.partial(_rank_kernel, blk=blk),
        grid=(n // tb,),
        in_specs=[pl.BlockSpec((2, tb), lambda i: (0, i)),
                  pl.BlockSpec(cnt.shape, lambda i: (0, 0))],
        out_specs=[pl.BlockSpec((2, tb), lambda i: (0, i)),
                   pl.BlockSpec((1, nbp), lambda i: (0, 0)),
                   pl.BlockSpec((1, LANES), lambda i: (0, 0))],
        out_shape=[jax.ShapeDtypeStruct((2, n), I32),
                   jax.ShapeDtypeStruct((1, nbp), I32),
                   jax.ShapeDtypeStruct((1, LANES), I32)],
        scratch_shapes=[pltpu.VMEM((N_EXPERTS, LANES), F32), pltpu.VMEM((N_EXPERTS, LANES), F32)],
        compiler_params=_params("arbitrary"),
        name="rank",
    )(eidx, cnt)


def _tile_copy(src, src_row, dst, dst_row, sem):
    return pltpu.make_async_copy(src.at[pl.ds(pl.multiple_of(src_row * SUBLANES, SUBLANES), SUBLANES), :],
                                 dst.at[pl.ds(pl.multiple_of(dst_row * SUBLANES, SUBLANES), SUBLANES), :], sem)


def _dispatch_kernel(dest_ref, h_hbm, xs_in_hbm, xs_hbm, sem, *, tb):
    del xs_in_hbm
    base = pl.program_id(0) * tb

    def issue(t, carry):
        for k in range(2):
            _tile_copy(h_hbm, base + t, xs_hbm, dest_ref[k, t], sem).start()
        return carry

    lax.fori_loop(0, tb, issue, 0)

    def drain(t, carry):
        for k in range(2):
            _tile_copy(h_hbm, 0, xs_hbm, 0, sem).wait()
        return carry

    lax.fori_loop(0, tb, drain, 0)


def _dispatch(dest, h2t, m_rows, tb=512):
    n = dest.shape[1]
    tb = min(tb, n)
    xs0 = jnp.zeros((m_rows * SUBLANES, LANES), F32)
    return pl.pallas_call(
        functools.partial(_dispatch_kernel, tb=tb),
        grid=(n // tb,),
        in_specs=[pl.BlockSpec((2, tb), lambda i: (0, i), memory_space=pltpu.SMEM),
                  pl.BlockSpec(memory_space=pl.ANY),
                  pl.BlockSpec(memory_space=pl.ANY)],
        out_specs=pl.BlockSpec(memory_space=pl.ANY),
        out_shape=jax.ShapeDtypeStruct(xs0.shape, F32),
        scratch_shapes=[pltpu.SemaphoreType.DMA],
        input_output_aliases={2: 0},
        compiler_params=pltpu.CompilerParams(dimension_semantics=("arbitrary",), has_side_effects=True),
        name="dispatch",
    )(dest, h2t, xs0)


def _expert_kernel(be_ref, nu_ref, xs_ref, w1_ref, w3_ref, w2_ref, y_ref, xb_scr, *, blk):
    del be_ref
    j = pl.program_id(0)
    nchunk = xb_scr.shape[1] // LANES

    @pl.when(j < nu_ref[0])
    def _():
        for c in range(nchunk):
            xb_scr[:, c * LANES:(c + 1) * LANES] = xs_ref[pl.ds(c, blk, stride=SUBLANES), :].astype(BF16)
        xb = xb_scr[...]
        a = jnp.dot(xb, w1_ref[...], preferred_element_type=F32)
        b = jnp.dot(xb, w3_ref[...], preferred_element_type=F32)
        hid = (a / (1.0 + jnp.exp(-a))) * b
        y = jnp.dot(hid.astype(BF16), w2_ref[...], preferred_element_type=F32)
        for c in range(nchunk):
            y_ref[pl.ds(c, blk, stride=SUBLANES), :] = y[:, c * LANES:(c + 1) * LANES]

    @pl.when(j >= nu_ref[0])
    def _():
        y_ref[...] = jnp.zeros_like(y_ref)


def _experts(block_e, n_used, xs, w1, w3, w2, n_blocks, blk):
    d, hid = w1.shape[1], w1.shape[2]
    grid_spec = pltpu.PrefetchScalarGridSpec(
        num_scalar_prefetch=2,
        grid=(n_blocks,),
        in_specs=[pl.BlockSpec((blk * SUBLANES, LANES), lambda j, be, nu: (j, 0)),
                  pl.BlockSpec((None, d, hid), lambda j, be, nu: (be[j], 0, 0)),
                  pl.BlockSpec((None, d, hid), lambda j, be, nu: (be[j], 0, 0)),
                  pl.BlockSpec((None, hid, d), lambda j, be, nu: (be[j], 0, 0))],
        out_specs=pl.BlockSpec((blk * SUBLANES, LANES), lambda j, be, nu: (j, 0)),
        scratch_shapes=[pltpu.VMEM((blk, d), BF16)],
    )
    return pl.pallas_call(
        functools.partial(_expert_kernel, blk=blk),
        grid_spec=grid_spec,
        out_shape=jax.ShapeDtypeStruct(xs.shape, F32),
        compiler_params=_params("arbitrary"),
        name="experts",
    )(block_e, n_used, xs, w1, w3, w2)


def _combine_kernel(dest_ref, x_ref, w_ref, gf_ref, y_hbm, xo_ref, g0_scr, g1_scr, sem, *, tb):
    bufs = (g0_scr, g1_scr)

    def issue(t, carry):
        for k in range(2):
            _tile_copy(y_hbm, dest_ref[k, t], bufs[k], t, sem).start()
        return carry

    lax.fori_loop(0, tb, issue, 0)

    def drain(t, carry):
        for k in range(2):
            _tile_copy(y_hbm, 0, bufs[k], 0, sem).wait()
        return carry

    lax.fori_loop(0, tb, drain, 0)
    w0 = w_ref[:, 0:1]
    w1 = w_ref[:, 1:2]
    for c in range(x_ref.shape[1] // LANES):
        sl = slice(c * LANES, (c + 1) * LANES)
        yc = w0 * g0_scr[pl.ds(c, tb, stride=SUBLANES), :] + w1 * g1_scr[pl.ds(c, tb, stride=SUBLANES), :]
        xo_ref[:, sl] = x_ref[:, sl] + gf_ref[:, sl] * yc


def _combine(dest, x2, wts_t, gate_f, y, seq, tb=256):
    n, d = x2.shape
    spb = seq // tb
    return pl.pallas_call(
        functools.partial(_combine_kernel, tb=tb),
        grid=(n // tb,),
        in_specs=[pl.BlockSpec((2, tb), lambda i: (0, i), memory_space=pltpu.SMEM),
                  pl.BlockSpec((tb, d), lambda i: (i, 0)),
                  pl.BlockSpec((tb, 2), lambda i: (i, 0)),
                  pl.BlockSpec((None, 1, d), lambda i: (i // spb, 0, 0)),
                  pl.BlockSpec(memory_space=pl.ANY)],
        out_specs=pl.BlockSpec((tb, d), lambda i: (i, 0)),
        out_shape=jax.ShapeDtypeStruct((n, d), F32),
        scratch_shapes=[pltpu.VMEM((tb * SUBLANES, LANES), F32), pltpu.VMEM((tb * SUBLANES, LANES), F32),
                        pltpu.SemaphoreType.DMA],
        compiler_params=_params("arbitrary"),
        name="combine",
    )(dest, x2, wts_t, gate_f, y)


def _final_norm_kernel(x_ref, g_ref, o_ref):
    x = x_ref[...]
    ms = jnp.mean(x * x, axis=-1, keepdims=True)
    o_ref[...] = x * lax.rsqrt(ms + NORM_EPS) * g_ref[...]


def _final_norm(x2, g, tm=512):
    n, d = x2.shape
    return pl.pallas_call(
        _final_norm_kernel,
        grid=(n // tm,),
        in_specs=[pl.BlockSpec((tm, d), lambda i: (i, 0)), pl.BlockSpec((1, d), lambda i: (0, 0))],
        out_specs=pl.BlockSpec((tm, d), lambda i: (i, 0)),
        out_shape=jax.ShapeDtypeStruct((n, d), F32),
        compiler_params=_params("arbitrary"),
        name="final_norm",
    )(x2, g)


def _rotary_tables(seq, width):
    inv = 1.0 / (ROPE_THETA ** (jnp.arange(0, HEAD_DIM, 2, dtype=F32) / HEAD_DIM))
    ang = jnp.arange(seq, dtype=F32)[:, None] * inv[None, :]
    ang = jnp.concatenate([ang, ang], axis=-1)
    sign = jnp.concatenate([-jnp.ones((HEAD_DIM // 2,), F32), jnp.ones((HEAD_DIM // 2,), F32)])
    reps = width // HEAD_DIM
    return jnp.tile(jnp.cos(ang), (1, reps)), jnp.tile(jnp.sin(ang) * sign[None, :], (1, reps))


def kernel(x, c, ada_w, ada_b, mix_norm_g, ffn_norm_g, w_in, da_lambda, da_subln_g, na_rpb, w_proj_a, w_proj_n, w_proj_d, w_out, router_group_w, router_group_b, router_expert_w, router_expert_b, expert_w1, expert_w3, expert_w2, final_norm_g):
    bsz, seq, d = x.shape
    depth = ada_w.shape[0]
    n = bsz * seq
    tn = 512
    assert w_in.shape[2] == NC

    mods = _ada(c, ada_w, ada_b).reshape(depth, bsz, 6, 1, d)
    cos_t, sin_t = _rotary_tables(seq, LANES)
    perm = jnp.concatenate([jnp.arange(lo, hi) for lo, hi in IN_PERM])
    scale = HEAD_DIM ** -0.5
    colscale = jnp.ones((1, NC), F32)
    for lo, hi in ((C_QA, C_KA), (C_QD, C_KD), (C_QN, C_KN)):
        colscale = colscale.at[:, lo:hi].set(scale)

    n_blocks = (2 * n) // MOE_BLK + N_EXPERTS
    m_rows = n_blocks * MOE_BLK
    rpad = LANES - N_EXPERTS - N_GROUPS

    x2 = x.reshape(n, d)
    for l in range(depth):
        lam_init = 0.8 - 0.6 * math.exp(-0.3 * l)
        shift_m, scale_m, gate_m, shift_f, scale_f, gate_f = (mods[l, :, k] for k in range(6))
        w_bf = w_in[l][:, perm].astype(BF16)
        z = _inproj(x2, mix_norm_g[l].reshape(1, d), scale_m, shift_m, cos_t, sin_t, colscale, w_bf, seq, tn=tn)
        o_a = _diff_attn(z, da_lambda[l], da_subln_g[l].reshape(1, 2 * HEAD_DIM), lam_init, bsz, seq)
        o_n = _na_attn(z, _na_bias_table(na_rpb[l]), bsz, seq)
        dil = [_dil_attn(z, g, bsz, seq) for g in range(len(DIL_DILATIONS))]
        wr = jnp.concatenate([router_expert_w[l], router_group_w[l], jnp.zeros((d, rpad), F32)], axis=1)
        br = jnp.concatenate([router_expert_b[l], router_group_b[l], jnp.zeros((rpad,), F32)]).reshape(1, LANES)
        x2, h2t, logits = _mix_out(
            x2, z, o_a, o_n, [o for o, _ in dil], [s for _, s in dil],
            w_proj_a[l].astype(BF16), w_proj_n[l].astype(BF16), w_proj_d[l].astype(BF16), w_out[l].astype(BF16),
            gate_m, ffn_norm_g[l].reshape(1, d), scale_f, shift_f, wr, br, seq)
        lgt = logits[:, :N_EXPERTS + SUBLANES].T
        eidx, wts, cnt = _route(lgt)
        dest, block_e, n_used = _rank(eidx, cnt, n_blocks, MOE_BLK)
        xs = _dispatch(dest, h2t, m_rows)
        y = _experts(block_e[0, :n_blocks], n_used[0, :1], xs,
                     expert_w1[l].astype(BF16), expert_w3[l].astype(BF16), expert_w2[l].astype(BF16),
                     n_blocks, MOE_BLK)
        x2 = _combine(dest, x2, wts.T, gate_f, y, seq)
    return _final_norm(x2, final_norm_g.reshape(1, d)).reshape(bsz, seq, d)
```

```python
import functools
import math

import jax
import jax.numpy as jnp
from jax import lax
from jax.experimental import pallas as pl
from jax.experimental.pallas import tpu as pltpu

F32 = jnp.float32
BF16 = jnp.bfloat16
I32 = jnp.int32
HIGHEST = lax.Precision.HIGHEST

HEAD_DIM = 64
ROPE_THETA = 10000.0
NORM_EPS = 1e-6
NEG_INF = -1e30

DA_HEADS = 4
GRID_W = 64
NA_HEADS = 4
NA_WIN_ROWS = 8
NA_WIN_COLS = 16
DIL_DILATIONS = (1, 4, 16)
DIL_SIDE = 64
DIL_HEADS = 4
N_GROUPS = 4
EXPERTS_PER_GROUP = 8
N_EXPERTS = N_GROUPS * EXPERTS_PER_GROUP

LANES = 128
SUBLANES = 8
MOE_BLK = 256
VMEM_LIMIT = 56 * 1024 * 1024

C_QA, C_KA, C_QD, C_KD = 0, 512, 1024, 1792
N_ROT = 2560
C_VA, C_GATE, C_QN, C_KN, C_VN, C_VD = 2560, 3072, 6144, 6400, 6656, 6912
NC = 7680
IN_PERM = ((0, 1024), (2304, 3840), (1024, 1536), (4608, 7680), (1536, 2304), (3840, 4608))


def _nt_dot(a, b):
    return lax.dot_general(a, b, (((1,), (1,)), ((), ())), preferred_element_type=F32)


def _stack_heads(q, nh):
    lane = lax.broadcasted_iota(I32, q.shape, 1)
    masks = [(lane >= HEAD_DIM * h) & (lane < HEAD_DIM * (h + 1)) for h in range(nh)]
    return jnp.concatenate([jnp.where(hm, q, jnp.zeros_like(q)) for hm in masks], axis=0), masks


def _unstack_heads(x, masks):
    m = x.shape[0] // len(masks)
    out = x[0:m]
    for h in range(1, len(masks)):
        out = jnp.where(masks[h], x[h * m:(h + 1) * m], out)
    return out


def _params(*sem):
    return pltpu.CompilerParams(dimension_semantics=sem, vmem_limit_bytes=VMEM_LIMIT)


def _ada_kernel(c_ref, w_ref, b_ref, o_ref):
    c = c_ref[...]
    ca = c / (1.0 + jnp.exp(-c))
    o_ref[...] = jnp.dot(ca, w_ref[...], preferred_element_type=F32, precision=HIGHEST) + b_ref[...]


def _ada(c, ada_w, ada_b):
    depth, d, w6 = ada_w.shape
    bsz = c.shape[0]
    rows = -(-bsz // SUBLANES) * SUBLANES
    cp = jnp.zeros((rows, d), F32).at[:bsz].set(c)
    tn = w6 // 4
    out = pl.pallas_call(
        _ada_kernel,
        grid=(depth, w6 // tn),
        in_specs=[pl.BlockSpec((rows, d), lambda l, j: (0, 0)),
                  pl.BlockSpec((None, d, tn), lambda l, j: (l, 0, j)),
                  pl.BlockSpec((None, 1, tn), lambda l, j: (l, 0, j))],
        out_specs=pl.BlockSpec((None, rows, tn), lambda l, j: (l, 0, j)),
        out_shape=jax.ShapeDtypeStruct((depth, rows, w6), F32),
        compiler_params=_params("arbitrary", "arbitrary"),
        name="ada_mod",
    )(cp, ada_w, ada_b.reshape(depth, 1, w6))
    return out[:, :bsz]


def _inproj_kernel(x_ref, g_ref, sc_ref, sh_ref, cos_ref, sin_ref, cs_ref, w_hbm, z_ref, h_scr, w_scr, sem, *, tn):
    @pl.when(pl.program_id(0) == 0)
    def _():
        cp = pltpu.make_async_copy(w_hbm, w_scr, sem)
        cp.start()
        cp.wait()

    x = x_ref[...]
    ms = jnp.mean(x * x, axis=-1, keepdims=True)
    y = x * lax.rsqrt(ms + NORM_EPS) * g_ref[...]
    h_scr[...] = (y * (1.0 + sc_ref[...]) + sh_ref[...]).astype(BF16)

    half = HEAD_DIM // 2
    lane = lax.broadcasted_iota(I32, (x.shape[0], LANES), 1)
    lo = (lane % HEAD_DIM) < half
    for j in range(w_scr.shape[1] // tn):
        cols = slice(j * tn, (j + 1) * tn)
        acc = jnp.dot(h_scr[...], w_scr[:, cols], preferred_element_type=F32) * cs_ref[:, cols]
        if j < N_ROT // tn:
            cos = cos_ref[...]
            sin = sin_ref[...]
            for cc in range(tn // LANES):
                a = acc[:, cc * LANES:(cc + 1) * LANES]
                rot = jnp.where(lo, pltpu.roll(a, LANES - half, 1), pltpu.roll(a, half, 1))
                c0 = j * tn + cc * LANES
                z_ref[:, c0:c0 + LANES] = (a * cos + rot * sin).astype(BF16)
        else:
            z_ref[:, cols] = acc.astype(BF16)


def _inproj(x2, g, scale, shift, cos_t, sin_t, colscale, w_bf, seq, tm=512, tn=512):
    n, d = x2.shape
    nc = w_bf.shape[1]
    assert N_ROT % tn == 0 and nc % tn == 0
    spb = seq // tm
    return pl.pallas_call(
        functools.partial(_inproj_kernel, tn=tn),
        grid=(n // tm,),
        in_specs=[pl.BlockSpec((tm, d), lambda i: (i, 0)),
                  pl.BlockSpec((1, d), lambda i: (0, 0)),
                  pl.BlockSpec((None, 1, d), lambda i: (i // spb, 0, 0)),
                  pl.BlockSpec((None, 1, d), lambda i: (i // spb, 0, 0)),
                  pl.BlockSpec((tm, LANES), lambda i: (i % spb, 0)),
                  pl.BlockSpec((tm, LANES), lambda i: (i % spb, 0)),
                  pl.BlockSpec((1, nc), lambda i: (0, 0)),
                  pl.BlockSpec(memory_space=pl.ANY)],
        out_specs=pl.BlockSpec((tm, nc), lambda i: (i, 0)),
        out_shape=jax.ShapeDtypeStruct((n, nc), BF16),
        scratch_shapes=[pltpu.VMEM((tm, d), BF16), pltpu.VMEM((d, nc), BF16), pltpu.SemaphoreType.DMA],
        compiler_params=_params("arbitrary"),
        name="inproj",
    )(x2, g, scale, shift, cos_t, sin_t, colscale, w_bf)


def _diff_attn_kernel(lam_ref, g_ref, q_ref, k_ref, v_ref, o_ref, *, lam_init, chunk):
    lp = lam_ref[...]
    lam = (jnp.exp(jnp.sum(lp[0:1] * lp[1:2], axis=-1, keepdims=True))
           - jnp.exp(jnp.sum(lp[2:3] * lp[3:4], axis=-1, keepdims=True)) + lam_init)
    k = k_ref[...]
    v = v_ref[...]
    lane = lax.broadcasted_iota(I32, (chunk, 2 * HEAD_DIM), 1)
    chains = [(c, m) for c in range(q_ref.shape[0] // chunk) for m in range(2)]

    def scores(c, m):
        q = q_ref[c * chunk:(c + 1) * chunk, :]
        qm = jnp.where((lane >= HEAD_DIM * m) & (lane < HEAD_DIM * (m + 1)), q, jnp.zeros_like(q))
        return _nt_dot(qm, k)

    s_next = scores(*chains[0])
    outs = {}
    for i, (c, m) in enumerate(chains):
        s = s_next
        if i + 1 < len(chains):
            s_next = scores(*chains[i + 1])
        mx = jnp.max(s, axis=-1, keepdims=True)
        p = jnp.exp(s - mx)
        l = jnp.sum(p, axis=-1, keepdims=True)
        outs[m] = jnp.dot(p.astype(BF16), v, preferred_element_type=F32) / l
        if m == 1:
            o = outs[0] - lam * outs[1]
            ms = jnp.mean(o * o, axis=-1, keepdims=True)
            o = o * lax.rsqrt(ms + NORM_EPS) * g_ref[...] * (1.0 - lam_init)
            o_ref[c * chunk:(c + 1) * chunk, :] = o.astype(BF16)


def _diff_attn(z, lam_p, subln_g, lam_init, bsz, seq, tq=512, chunk=256):
    n = z.shape[0]
    hw = 2 * HEAD_DIM
    nq = seq // tq
    return pl.pallas_call(
        functools.partial(_diff_attn_kernel, lam_init=lam_init, chunk=min(chunk, tq)),
        grid=(bsz, DA_HEADS, nq),
        in_specs=[pl.BlockSpec((4, HEAD_DIM), lambda b, h, i: (0, 0)),
                  pl.BlockSpec((1, hw), lambda b, h, i: (0, 0)),
                  pl.BlockSpec((tq, hw), lambda b, h, i: (b * nq + i, C_QA // hw + h)),
                  pl.BlockSpec((seq, hw), lambda b, h, i: (b, C_KA // hw + h)),
                  pl.BlockSpec((seq, hw), lambda b, h, i: (b, C_VA // hw + h))],
        out_specs=pl.BlockSpec((tq, hw), lambda b, h, i: (b * nq + i, h)),
        out_shape=jax.ShapeDtypeStruct((n, DA_HEADS * hw), BF16),
        compiler_params=_params("arbitrary", "arbitrary", "arbitrary"),
        name="diff_attn",
    )(lam_p, subln_g, z, z, z)


def _na_kernel(q_ref, k_ref, v_ref, bias_ref, o_ref, *, rows):
    win = NA_WIN_ROWS * GRID_W

    def body(r, carry):
        kr0 = jnp.clip(r - NA_WIN_ROWS // 2, 0, rows - NA_WIN_ROWS)
        pat = kr0 - r + (NA_WIN_ROWS - 1)
        q4, masks = _stack_heads(q_ref[pl.ds(pl.multiple_of(r * GRID_W, GRID_W), GRID_W), :], NA_HEADS)
        kw = k_ref[pl.ds(pl.multiple_of(kr0 * GRID_W, GRID_W), win), :]
        vw = v_ref[pl.ds(pl.multiple_of(kr0 * GRID_W, GRID_W), win), :]
        s = _nt_dot(q4, kw) + bias_ref[pat]
        mx = jnp.max(s, axis=-1, keepdims=True)
        p = jnp.exp(s - mx)
        l = jnp.sum(p, axis=-1, keepdims=True)
        o4 = jnp.dot(p.astype(BF16), vw, preferred_element_type=F32) / l
        o_ref[pl.ds(pl.multiple_of(r * GRID_W, GRID_W), GRID_W), :] = _unstack_heads(o4, masks).astype(BF16)
        return carry

    lax.fori_loop(0, rows, body, 0, unroll=2)


def _na_bias_table(rpb):
    pat = jnp.arange(NA_WIN_ROWS)[:, None]
    krl = jnp.arange(NA_WIN_ROWS)[None, :]
    ri = pat + krl
    c = jnp.arange(GRID_W)[:, None]
    kc = jnp.arange(GRID_W)[None, :]
    kc0 = jnp.clip(c - NA_WIN_COLS // 2, 0, GRID_W - NA_WIN_COLS)
    valid = (kc >= kc0) & (kc < kc0 + NA_WIN_COLS)
    ci = jnp.clip(kc - c + (NA_WIN_COLS - 1), 0, 2 * NA_WIN_COLS - 2)
    tab = rpb[:, ri[:, :, None, None], ci[None, None, :, :]]
    tab = jnp.where(valid[None, None, None], tab.astype(F32), NEG_INF)
    tab = tab.transpose(1, 0, 3, 2, 4)
    return tab.reshape(NA_WIN_ROWS, NA_HEADS * GRID_W, NA_WIN_ROWS * GRID_W)


def _na_attn(z, bias_tab, bsz, seq):
    n = z.shape[0]
    width = NA_HEADS * HEAD_DIM
    rows = seq // GRID_W
    assert rows >= NA_WIN_ROWS
    return pl.pallas_call(
        functools.partial(_na_kernel, rows=rows),
        grid=(bsz,),
        in_specs=[pl.BlockSpec((seq, width), lambda b: (b, C_QN // width)),
                  pl.BlockSpec((seq, width), lambda b: (b, C_KN // width)),
                  pl.BlockSpec((seq, width), lambda b: (b, C_VN // width)),
                  pl.BlockSpec(bias_tab.shape, lambda b: (0, 0, 0))],
        out_specs=pl.BlockSpec((seq, width), lambda b: (b, 0)),
        out_shape=jax.ShapeDtypeStruct((n, width), BF16),
        compiler_params=_params("arbitrary"),
        name="na_attn",
    )(z, z, z, bias_tab)


def _dil_kernel(q_ref, k_ref, v_ref, o_ref, lse_ref, *scratch, dil, seg, win, tq):
    width = DIL_HEADS * HEAD_DIM
    nlc = width // LANES
    qi = pl.program_id(1)
    if dil > 1:
        kd_scr, vd_scr, stage_scr, ostage_scr, lstage_scr = scratch

        @pl.when(qi == 0)
        def _():
            for src, dst in ((k_ref, kd_scr), (v_ref, vd_scr)):
                for lc in range(nlc):
                    stage_scr[lc] = src[:, lc * LANES:(lc + 1) * LANES].astype(F32)
                for r in range(dil):
                    for lc in range(nlc):
                        dst[r * seg:(r + 1) * seg, lc * LANES:(lc + 1) * LANES] = (
                            stage_scr[lc, pl.ds(r, seg, stride=dil), :].astype(BF16))

        for lc in range(nlc):
            stage_scr[lc, 0:tq * dil, :] = q_ref[:, lc * LANES:(lc + 1) * LANES].astype(F32)
        k_src, v_src = kd_scr, vd_scr
    else:
        k_src, v_src = k_ref, v_ref
    a0 = qi * tq
    w0 = jnp.clip(a0 - DIL_SIDE, 0, seg - win)
    aq = a0 + lax.broadcasted_iota(I32, (DIL_HEADS * tq, win), 0) % tq
    ak = w0 + lax.broadcasted_iota(I32, (DIL_HEADS * tq, win), 1)
    valid = jnp.abs(aq - ak) <= DIL_SIDE

    def residue(r):
        if dil > 1:
            q = jnp.concatenate([stage_scr[lc, pl.ds(r, tq, stride=dil), :] for lc in range(nlc)], axis=1).astype(BF16)
        else:
            q = q_ref[...]
        start = pl.multiple_of(r * seg + w0, DIL_SIDE)
        kw = k_src[pl.ds(start, win), :]
        vw = v_src[pl.ds(start, win), :]
        q4, masks = _stack_heads(q, DIL_HEADS)
        s = jnp.where(valid, _nt_dot(q4, kw), NEG_INF)
        mx = jnp.max(s, axis=-1, keepdims=True)
        p = jnp.exp(s - mx)
        l = jnp.sum(p, axis=-1, keepdims=True)
        o = _unstack_heads(jnp.dot(p.astype(BF16), vw, preferred_element_type=F32) / l, masks)
        lse = _unstack_heads(jnp.broadcast_to(mx + jnp.log(l), (DIL_HEADS * tq, width)), masks)
        if dil > 1:
            for lc in range(nlc):
                ostage_scr[lc, pl.ds(r, tq, stride=dil), :] = o[:, lc * LANES:(lc + 1) * LANES]
                lstage_scr[lc, pl.ds(r, tq, stride=dil), :] = lse[:, lc * LANES:(lc + 1) * LANES]
        else:
            o_ref[...] = o
            lse_ref[...] = lse

    if dil == 1:
        residue(0)
    else:
        def body(r, carry):
            residue(r)
            return carry

        lax.fori_loop(0, dil, body, 0, unroll=2)
        for lc in range(nlc):
            o_ref[:, lc * LANES:(lc + 1) * LANES] = ostage_scr[lc]
            lse_ref[:, lc * LANES:(lc + 1) * LANES] = lstage_scr[lc]


def _dil_attn(z, g, bsz, seq):
    dil = DIL_DILATIONS[g]
    n = z.shape[0]
    width = DIL_HEADS * HEAD_DIM
    seg = seq // dil
    tq = min(128, seg)
    win = min(2 * DIL_SIDE + tq, seg)
    nq = seg // tq
    cq, ck, cv = (C_QD // width + g, C_KD // width + g, C_VD // width + g)
    scratch = []
    if dil > 1:
        nlc = width // LANES
        scratch = [pltpu.VMEM((seq, width), BF16), pltpu.VMEM((seq, width), BF16),
                   pltpu.VMEM((nlc, seq, LANES), F32),
                   pltpu.VMEM((nlc, tq * dil, LANES), F32), pltpu.VMEM((nlc, tq * dil, LANES), F32)]
    return pl.pallas_call(
        functools.partial(_dil_kernel, dil=dil, seg=seg, win=win, tq=tq),
        grid=(bsz, nq),
        in_specs=[pl.BlockSpec((tq * dil, width), lambda b, i: (b * nq + i, cq)),
                  pl.BlockSpec((seq, width), lambda b, i: (b, ck)),
                  pl.BlockSpec((seq, width), lambda b, i: (b, cv))],
        out_specs=[pl.BlockSpec((tq * dil, width), lambda b, i: (b * nq + i, 0)),
                   pl.BlockSpec((tq * dil, width), lambda b, i: (b * nq + i, 0))],
        out_shape=[jax.ShapeDtypeStruct((n, width), F32), jax.ShapeDtypeStruct((n, width), F32)],
        scratch_shapes=scratch,
        compiler_params=_params("arbitrary", "arbitrary"),
        name=f"dil_attn_g{g}",
    )(z, z, z)


def _mix_out_kernel(x_ref, oa_ref, on_ref, od0_ref, od1_ref, od2_ref, l0_ref, l1_ref, l2_ref,
                    ga_ref, gn_ref, gd_ref, wpa_ref, wpn_ref, wpd_ref, wout_ref,
                    gm_ref, fg_ref, fsc_ref, fsh_ref, wr_ref, br_ref,
                    xo_ref, h2_ref, lg_ref):
    tm = x_ref.shape[0]
    l0, l1, l2 = l0_ref[...], l1_ref[...], l2_ref[...]
    mx = jnp.maximum(jnp.maximum(l0, l1), l2)
    e0, e1, e2 = jnp.exp(l0 - mx), jnp.exp(l1 - mx), jnp.exp(l2 - mx)
    den = e0 + e1 + e2
    od = (e0 / den) * od0_ref[...] + (e1 / den) * od1_ref[...] + (e2 / den) * od2_ref[...]

    def sig(ref):
        g = ref[...].astype(F32)
        return 1.0 / (1.0 + jnp.exp(-g))

    merged = (sig(ga_ref) * jnp.dot(oa_ref[...], wpa_ref[...], preferred_element_type=F32)
              + sig(gn_ref) * jnp.dot(on_ref[...], wpn_ref[...], preferred_element_type=F32)
              + sig(gd_ref) * jnp.dot(od.astype(BF16), wpd_ref[...], preferred_element_type=F32))
    y = jnp.dot(merged.astype(BF16), wout_ref[...], preferred_element_type=F32)
    xn = x_ref[...] + gm_ref[...] * y
    xo_ref[...] = xn
    ms = jnp.mean(xn * xn, axis=-1, keepdims=True)
    h2 = xn * lax.rsqrt(ms + NORM_EPS) * fg_ref[...]
    h2 = h2 * (1.0 + fsc_ref[...]) + fsh_ref[...]
    lg_ref[...] = lax.dot_general(wr_ref[...], h2, (((1,), (1,)), ((), ())), preferred_element_type=F32,
                                  precision=HIGHEST) + br_ref[...]
    for c in range(h2.shape[1] // LANES):
        h2_ref[pl.ds(c, tm, stride=SUBLANES), :] = h2[:, c * LANES:(c + 1) * LANES]


def _mix_out(x2, z, o_a, o_n, o_d, lse_d, wpa, wpn, wpd, wout, gate_m, fg, fsc, fsh, wr, br, seq, tm=256):
    n, d = x2.shape
    assert d // LANES == SUBLANES and C_GATE % d == 0
    spb = seq // tm
    gcol = C_GATE // d
    wd = DIL_HEADS * HEAD_DIM
    row = lambda w: pl.BlockSpec((tm, w), lambda i: (i, 0))
    full = lambda a: pl.BlockSpec(a.shape, lambda i: (0,) * a.ndim)
    perb = pl.BlockSpec((None, 1, d), lambda i: (i // spb, 0, 0))
    return pl.pallas_call(
        _mix_out_kernel,
        grid=(n // tm,),
        in_specs=[row(d), row(o_a.shape[1]), row(o_n.shape[1]), row(wd), row(wd), row(wd), row(wd), row(wd), row(wd),
                  pl.BlockSpec((tm, d), lambda i: (i, gcol)),
                  pl.BlockSpec((tm, d), lambda i: (i, gcol + 1)),
                  pl.BlockSpec((tm, d), lambda i: (i, gcol + 2)),
                  full(wpa), full(wpn), full(wpd), full(wout),
                  perb, full(fg), perb, perb, full(wr), full(br)],
        out_specs=[row(d), pl.BlockSpec((tm * SUBLANES, LANES), lambda i: (i, 0)),
                   pl.BlockSpec((wr.shape[0], tm), lambda i: (0, i))],
        out_shape=[jax.ShapeDtypeStruct((n, d), F32),
                   jax.ShapeDtypeStruct((n * SUBLANES, LANES), F32),
                   jax.ShapeDtypeStruct((wr.shape[0], n), F32)],
        compiler_params=_params("arbitrary"),
        name="mix_out",
    )(x2, o_a, o_n, o_d[0], o_d[1], o_d[2], lse_d[0], lse_d[1], lse_d[2], z, z, z,
      wpa, wpn, wpd, wout, gate_m, fg, fsc, fsh, wr, br)


def _route_kernel(lg_ref, e_ref, w_ref, cnt_ref):
    i = pl.program_id(0)
    le = lg_ref[0:N_EXPERTS, :]
    gl = lg_ref[N_EXPERTS:N_EXPERTS + N_GROUPS, :]
    tb = le.shape[1]
    gmax = jnp.max(gl, axis=0, keepdims=True)
    grow = lax.broadcasted_iota(I32, gl.shape, 0)
    gidx = jnp.min(jnp.where(gl == gmax, grow, N_GROUPS), axis=0, keepdims=True)
    pg_top = 1.0 / jnp.sum(jnp.exp(gl - gmax), axis=0, keepdims=True)
    erow = lax.broadcasted_iota(I32, le.shape, 0)
    lm = jnp.where((erow // EXPERTS_PER_GROUP) == gidx, le, -jnp.inf)
    m1 = jnp.max(lm, axis=0, keepdims=True)
    i1 = jnp.min(jnp.where(lm == m1, erow, N_EXPERTS), axis=0, keepdims=True)
    lm2 = jnp.where(erow == i1, -jnp.inf, lm)
    m2 = jnp.max(lm2, axis=0, keepdims=True)
    i2 = jnp.min(jnp.where(lm2 == m2, erow, N_EXPERTS), axis=0, keepdims=True)
    t = jnp.exp(m2 - m1)
    e_ref[0:1, :] = i1
    e_ref[1:2, :] = i2
    w_ref[0:1, :] = pg_top / (1.0 + t)
    w_ref[1:2, :] = pg_top * t / (1.0 + t)
    oh = (erow == i1).astype(F32) + (erow == i2).astype(F32)
    cnt = jnp.sum(oh, axis=1, keepdims=True)

    @pl.when(i == 0)
    def _():
        cnt_ref[...] = jnp.zeros_like(cnt_ref)

    cnt_ref[...] += jnp.broadcast_to(cnt, cnt_ref.shape)


def _route(lgt, tb=1024):
    rows, n = lgt.shape
    tb = min(tb, n)
    return pl.pallas_call(
        _route_kernel,
        grid=(n // tb,),
        in_specs=[pl.BlockSpec((rows, tb), lambda i: (0, i))],
        out_specs=[pl.BlockSpec((2, tb), lambda i: (0, i)),
                   pl.BlockSpec((2, tb), lambda i: (0, i)),
                   pl.BlockSpec((N_EXPERTS, LANES), lambda i: (0, 0))],
        out_shape=[jax.ShapeDtypeStruct((2, n), I32),
                   jax.ShapeDtypeStruct((2, n), F32),
                   jax.ShapeDtypeStruct((N_EXPERTS, LANES), F32)],
        compiler_params=_params("arbitrary"),
        name="route",
    )(lgt)


def _rank_kernel(e_ref, cnt_ref, dest_ref, be_ref, nu_ref, zs_ref, carry_scr, base_scr, *, blk):
    i = pl.program_id(0)

    @pl.when(i == 0)
    def _():
        cnt = cnt_ref[...]
        padded = jnp.floor((cnt + (blk - 1)) / blk) * blk
        r = lax.broadcasted_iota(I32, cnt.shape, 0)
        c = lax.broadcasted_iota(I32, cnt.shape, 1)
        prow = jnp.sum(jnp.where(r == c, padded, 0.0), axis=0, keepdims=True)
        pad_end = jnp.sum(jnp.where(c <= r, prow, 0.0), axis=1, keepdims=True)
        base = jnp.broadcast_to(pad_end, cnt.shape) - padded
        base_scr[...] = base
        carry_scr[...] = jnp.zeros_like(carry_scr)
        zs_ref[...] = jnp.sum(jnp.where(r == c, base + cnt, 0.0) + jnp.where(r + N_EXPERTS == c, padded - cnt, 0.0),
                              axis=0, keepdims=True).astype(I32)
        jb = lax.broadcasted_iota(I32, (N_EXPERTS, be_ref.shape[1]), 1).astype(F32) * blk
        be = jnp.sum((pad_end <= jb).astype(I32), axis=0, keepdims=True)
        be_ref[...] = jnp.minimum(be, N_EXPERTS - 1)
        nu_ref[...] = jnp.broadcast_to(jnp.max(pad_end, axis=0, keepdims=True) / blk, nu_ref.shape).astype(I32)

    e = e_ref[...]
    tb = e.shape[1]
    erow = lax.broadcasted_iota(I32, (N_EXPERTS, tb), 0)
    oh0 = erow == e[0:1]
    oh1 = erow == e[1:2]
    both = jnp.where(oh0 | oh1, 1.0, 0.0)
    upper = jnp.where(lax.broadcasted_iota(I32, (tb, tb), 0) < lax.broadcasted_iota(I32, (tb, tb), 1), 1.0, 0.0)
    prefix = jnp.dot(both.astype(BF16), upper.astype(BF16), preferred_element_type=F32)
    tot = prefix + base_scr[:, 0:1] + carry_scr[:, 0:1]
    dest_ref[0:1, :] = jnp.sum(jnp.where(oh0, tot, 0.0), axis=0, keepdims=True).astype(I32)
    dest_ref[1:2, :] = jnp.sum(jnp.where(oh1, tot, 0.0), axis=0, keepdims=True).astype(I32)
    carry_scr[...] += jnp.broadcast_to(jnp.sum(both, axis=1, keepdims=True), carry_scr.shape)


def _rank(eidx, cnt, n_blocks, blk, tb=512):
    n = eidx.shape[1]
    tb = min(tb, n)
    nbp = -(-n_blocks // LANES) * LANES
    return pl.pallas_call(
        functools.partial(_rank_kernel, blk=blk),
        grid=(n // tb,),
        in_specs=[pl.BlockSpec((2, tb), lambda i: (0, i)),
                  pl.BlockSpec(cnt.shape, lambda i: (0, 0))],
        out_specs=[pl.BlockSpec((2, tb), lambda i: (0, i)),
                   pl.BlockSpec((1, nbp), lambda i: (0, 0)),
                   pl.BlockSpec((1, LANES), lambda i: (0, 0)),
                   pl.BlockSpec((1, LANES), lambda i: (0, 0))],
        out_shape=[jax.ShapeDtypeStruct((2, n), I32),
                   jax.ShapeDtypeStruct((1, nbp), I32),
                   jax.ShapeDtypeStruct((1, LANES), I32),
                   jax.ShapeDtypeStruct((1, LANES), I32)],
        scratch_shapes=[pltpu.VMEM((N_EXPERTS, LANES), F32), pltpu.VMEM((N_EXPERTS, LANES), F32)],
        compiler_params=_params("arbitrary"),
        name="rank",
    )(eidx, cnt)


def _tile_copy(src, src_row, dst, dst_row, sem):
    return pltpu.make_async_copy(src.at[pl.ds(pl.multiple_of(src_row * SUBLANES, SUBLANES), SUBLANES), :],
                                 dst.at[pl.ds(pl.multiple_of(dst_row * SUBLANES, SUBLANES), SUBLANES), :], sem)


def _dispatch_kernel(zs_ref, nu_ref, dest_ref, h_ref, xs_hbm, zero_scr, sem, zsem, *, tb, blk, total_blocks):
    def zero_fill(slot, nslots):
        start = pl.multiple_of(slot * SUBLANES, SUBLANES)
        return pltpu.make_async_copy(zero_scr.at[pl.ds(0, nslots * SUBLANES), :],
                                     xs_hbm.at[pl.ds(start, nslots * SUBLANES), :], zsem)

    def pad_fills(act):
        for e in range(N_EXPERTS):
            off = zs_ref[e]
            count = zs_ref[N_EXPERTS + e]
            for b in range(blk.bit_length() - 1):
                bit = (count >> b) & 1

                @pl.when(bit == 1)
                def _(off=off, b=b):
                    act(zero_fill(off, 1 << b))

                off = off + bit * (1 << b)

    @pl.when(pl.program_id(0) == 0)
    def _():
        zero_scr[...] = jnp.zeros_like(zero_scr)
        pad_fills(lambda cp: cp.start())

        def tail_start(j, carry):
            zero_fill(j * blk, blk).start()
            return carry

        def tail_wait(j, carry):
            zero_fill(j * blk, blk).wait()
            return carry

        lax.fori_loop(nu_ref[0], total_blocks, tail_start, 0)
        pad_fills(lambda cp: cp.wait())
        lax.fori_loop(nu_ref[0], total_blocks, tail_wait, 0)

    def issue(t, carry):
        for k in range(2):
            _tile_copy(h_ref, t, xs_hbm, dest_ref[k, t], sem).start()
        return carry

    lax.fori_loop(0, tb, issue, 0)
    for k in range(2):
        pltpu.make_async_copy(h_ref, xs_hbm.at[pl.ds(0, tb * SUBLANES), :], sem).wait()


def _dispatch(zstart, n_used, dest, h2t, n_blocks, blk, tb=512):
    n = dest.shape[1]
    tb = min(tb, n)
    grid_spec = pltpu.PrefetchScalarGridSpec(
        num_scalar_prefetch=2,
        grid=(n // tb,),
        in_specs=[pl.BlockSpec((2, tb), lambda i, zs, nu: (0, i), memory_space=pltpu.SMEM),
                  pl.BlockSpec((tb * SUBLANES, LANES), lambda i, zs, nu: (i, 0))],
        out_specs=pl.BlockSpec(memory_space=pl.ANY),
        scratch_shapes=[pltpu.VMEM((blk * SUBLANES, LANES), F32), pltpu.SemaphoreType.DMA, pltpu.SemaphoreType.DMA],
    )
    total_blocks = n_blocks
    return pl.pallas_call(
        functools.partial(_dispatch_kernel, tb=tb, blk=blk, total_blocks=total_blocks),
        grid_spec=grid_spec,
        out_shape=jax.ShapeDtypeStruct((total_blocks * blk * SUBLANES, LANES), F32),
        compiler_params=pltpu.CompilerParams(dimension_semantics=("arbitrary",), has_side_effects=True),
        name="dispatch",
    )(zstart, n_used, dest, h2t)


def _expert_kernel(be_ref, first_ref, nu_ref, xs_ref, w1_ref, w3_ref, w2_ref, y_ref, xb_scr, w1_scr, w3_scr, w2_scr, *, blk):
    del be_ref
    j = pl.program_id(0)
    nchunk = xb_scr.shape[1] // LANES

    @pl.when(j < nu_ref[0])
    def _():
        @pl.when(first_ref[j] == 1)
        def _():
            w1_scr[...] = w1_ref[...].astype(BF16)
            w3_scr[...] = w3_ref[...].astype(BF16)
            w2_scr[...] = w2_ref[...].astype(BF16)

        for c in range(nchunk):
            xb_scr[:, c * LANES:(c + 1) * LANES] = xs_ref[pl.ds(c, blk, stride=SUBLANES), :].astype(BF16)
        xb = xb_scr[...]
        a = jnp.dot(xb, w1_scr[...], preferred_element_type=F32)
        b = jnp.dot(xb, w3_scr[...], preferred_element_type=F32)
        hid = (a / (1.0 + jnp.exp(-a))) * b
        y = jnp.dot(hid.astype(BF16), w2_scr[...], preferred_element_type=F32)
        for c in range(nchunk):
            y_ref[pl.ds(c, blk, stride=SUBLANES), :] = y[:, c * LANES:(c + 1) * LANES]

    @pl.when(j >= nu_ref[0])
    def _():
        y_ref[...] = jnp.zeros_like(y_ref)


def _experts(block_e, first, n_used, xs, w1, w3, w2, n_blocks, blk):
    d, hid = w1.shape[1], w1.shape[2]
    used = lambda j, nu: jnp.minimum(j, nu[0] - 1)
    grid_spec = pltpu.PrefetchScalarGridSpec(
        num_scalar_prefetch=3,
        grid=(n_blocks,),
        in_specs=[pl.BlockSpec((blk * SUBLANES, LANES), lambda j, be, fi, nu: (used(j, nu), 0)),
                  pl.BlockSpec((None, d, hid), lambda j, be, fi, nu: (be[j], 0, 0)),
                  pl.BlockSpec((None, d, hid), lambda j, be, fi, nu: (be[j], 0, 0)),
                  pl.BlockSpec((None, hid, d), lambda j, be, fi, nu: (be[j], 0, 0))],
        out_specs=pl.BlockSpec((blk * SUBLANES, LANES), lambda j, be, fi, nu: (j, 0)),
        scratch_shapes=[pltpu.VMEM((blk, d), BF16), pltpu.VMEM((d, hid), BF16), pltpu.VMEM((d, hid), BF16),
                        pltpu.VMEM((hid, d), BF16)],
    )
    return pl.pallas_call(
        functools.partial(_expert_kernel, blk=blk),
        grid_spec=grid_spec,
        out_shape=jax.ShapeDtypeStruct((n_blocks * blk * SUBLANES, LANES), F32),
        compiler_params=_params("arbitrary"),
        name="experts",
    )(block_e, first, n_used, xs, w1, w3, w2)


def _combine_kernel(dcur_ref, dnext_ref, x_ref, w_ref, gf_ref, y_hbm, xo_ref, g_scr, sems, *, tb, nsteps):
    i = pl.program_id(0)
    slot = i % 2

    def gather(d_ref, s):
        def issue(t, carry):
            for k in range(2):
                _tile_copy(y_hbm, d_ref[k, t], g_scr.at[s, k], t, sems.at[s]).start()
            return carry

        lax.fori_loop(0, tb, issue, 0)

    @pl.when(i == 0)
    def _():
        gather(dcur_ref, 0)

    @pl.when(i + 1 < nsteps)
    def _():
        gather(dnext_ref, 1 - slot)

    for k in range(2):
        pltpu.make_async_copy(y_hbm.at[pl.ds(0, tb * SUBLANES), :], g_scr.at[slot, k], sems.at[slot]).wait()
    w0 = w_ref[:, 0:1]
    w1 = w_ref[:, 1:2]
    for c in range(x_ref.shape[1] // LANES):
        sl = slice(c * LANES, (c + 1) * LANES)
        yc = (w0 * g_scr[slot, 0, pl.ds(c, tb, stride=SUBLANES), :]
              + w1 * g_scr[slot, 1, pl.ds(c, tb, stride=SUBLANES), :])
        xo_ref[:, sl] = x_ref[:, sl] + gf_ref[:, sl] * yc


def _combine(dest, x2, wts_t, gate_f, y, seq, tb=256):
    n, d = x2.shape
    spb = seq // tb
    nsteps = n // tb
    return pl.pallas_call(
        functools.partial(_combine_kernel, tb=tb, nsteps=nsteps),
        grid=(nsteps,),
        in_specs=[pl.BlockSpec((2, tb), lambda i: (0, i), memory_space=pltpu.SMEM),
                  pl.BlockSpec((2, tb), lambda i: (0, jnp.minimum(i + 1, nsteps - 1)), memory_space=pltpu.SMEM),
                  pl.BlockSpec((tb, d), lambda i: (i, 0)),
                  pl.BlockSpec((tb, 2), lambda i: (i, 0)),
                  pl.BlockSpec((None, 1, d), lambda i: (i // spb, 0, 0)),
                  pl.BlockSpec(memory_space=pl.ANY)],
        out_specs=pl.BlockSpec((tb, d), lambda i: (i, 0)),
        out_shape=jax.ShapeDtypeStruct((n, d), F32),
        scratch_shapes=[pltpu.VMEM((2, 2, tb * SUBLANES, LANES), F32), pltpu.SemaphoreType.DMA((2,))],
        compiler_params=_params("arbitrary"),
        name="combine",
    )(dest, dest, x2, wts_t, gate_f, y)


def _final_norm_kernel(x_ref, g_ref, o_ref):
    x = x_ref[...]
    ms = jnp.mean(x * x, axis=-1, keepdims=True)
    o_ref[...] = x * lax.rsqrt(ms + NORM_EPS) * g_ref[...]


def _final_norm(x2, g, tm=512):
    n, d = x2.shape
    return pl.pallas_call(
        _final_norm_kernel,
        grid=(n // tm,),
        in_specs=[pl.BlockSpec((tm, d), lambda i: (i, 0)), pl.BlockSpec((1, d), lambda i: (0, 0))],
        out_specs=pl.BlockSpec((tm, d), lambda i: (i, 0)),
        out_shape=jax.ShapeDtypeStruct((n, d), F32),
        compiler_params=_params("arbitrary"),
        name="final_norm",
    )(x2, g)


def _rotary_tables(seq, width):
    inv = 1.0 / (ROPE_THETA ** (jnp.arange(0, HEAD_DIM, 2, dtype=F32) / HEAD_DIM))
    ang = jnp.arange(seq, dtype=F32)[:, None] * inv[None, :]
    ang = jnp.concatenate([ang, ang], axis=-1)
    sign = jnp.concatenate([-jnp.ones((HEAD_DIM // 2,), F32), jnp.ones((HEAD_DIM // 2,), F32)])
    reps = width // HEAD_DIM
    return jnp.tile(jnp.cos(ang), (1, reps)), jnp.tile(jnp.sin(ang) * sign[None, :], (1, reps))


def kernel(x, c, ada_w, ada_b, mix_norm_g, ffn_norm_g, w_in, da_lambda, da_subln_g, na_rpb, w_proj_a, w_proj_n, w_proj_d, w_out, router_group_w, router_group_b, router_expert_w, router_expert_b, expert_w1, expert_w3, expert_w2, final_norm_g):
    bsz, seq, d = x.shape
    depth = ada_w.shape[0]
    n = bsz * seq
    tn = 512
    assert w_in.shape[2] == NC

    mods = _ada(c, ada_w, ada_b).reshape(depth, bsz, 6, 1, d)
    cos_t, sin_t = _rotary_tables(seq, LANES)
    scale = HEAD_DIM ** -0.5
    colscale = jnp.ones((1, NC), F32)
    for lo, hi in ((C_QA, C_KA), (C_QD, C_KD), (C_QN, C_KN)):
        colscale = colscale.at[:, lo:hi].set(scale)

    n_blocks = (2 * n) // MOE_BLK + N_EXPERTS
    rpad = SUBLANES - N_GROUPS

    x2 = x.reshape(n, d)
    for l in range(depth):
        lam_init = 0.8 - 0.6 * math.exp(-0.3 * l)
        shift_m, scale_m, gate_m, shift_f, scale_f, gate_f = (mods[l, :, k] for k in range(6))
        w_bf = jnp.concatenate([w_in[l, :, lo:hi] for lo, hi in IN_PERM], axis=1).astype(BF16)
        z = _inproj(x2, mix_norm_g[l].reshape(1, d), scale_m, shift_m, cos_t, sin_t, colscale, w_bf, seq, tn=tn)
        o_a = _diff_attn(z, da_lambda[l], da_subln_g[l].reshape(1, 2 * HEAD_DIM), lam_init, bsz, seq)
        o_n = _na_attn(z, _na_bias_table(na_rpb[l]), bsz, seq)
        dil = [_dil_attn(z, g, bsz, seq) for g in range(len(DIL_DILATIONS))]
        wr = jnp.concatenate([router_expert_w[l], router_group_w[l], jnp.zeros((d, rpad), F32)], axis=1).T
        br = jnp.concatenate([router_expert_b[l], router_group_b[l], jnp.zeros((rpad,), F32)]).reshape(-1, 1)
        x2, h2t, lgt = _mix_out(
            x2, z, o_a, o_n, [o for o, _ in dil], [s for _, s in dil],
            w_proj_a[l].astype(BF16), w_proj_n[l].astype(BF16), w_proj_d[l].astype(BF16), w_out[l].astype(BF16),
            gate_m, ffn_norm_g[l].reshape(1, d), scale_f, shift_f, wr, br, seq)
        eidx, wts, cnt = _route(lgt)
        dest, block_e, n_used, zstart = _rank(eidx, cnt, n_blocks, MOE_BLK)
        block_e = block_e[0, :n_blocks]
        first = jnp.concatenate([jnp.ones((1,), I32), (block_e[1:] != block_e[:-1]).astype(I32)])
        n_used = n_used[0, :1]
        xs = _dispatch(zstart[0, :2 * N_EXPERTS], n_used, dest, h2t, n_blocks, MOE_BLK)
        y = _experts(block_e, first, n_used, xs, expert_w1[l], expert_w3[l], expert_w2[l], n_blocks, MOE_BLK)
        x2 = _combine(dest, x2, wts.T, gate_f, y, seq)
    return _final_norm(x2, final_norm_g.reshape(1, d)).reshape(bsz, seq, d)
```

```python
import functools
import math

import jax
import jax.numpy as jnp
from jax import lax
from jax.experimental import pallas as pl
from jax.experimental.pallas import tpu as pltpu

F32 = jnp.float32
BF16 = jnp.bfloat16
I32 = jnp.int32
HIGHEST = lax.Precision.HIGHEST

HEAD_DIM = 64
ROPE_THETA = 10000.0
NORM_EPS = 1e-6
NEG_INF = -1e30

DA_HEADS = 4
GRID_W = 64
NA_HEADS = 4
NA_WIN_ROWS = 8
NA_WIN_COLS = 16
DIL_DILATIONS = (1, 4, 16)
DIL_SIDE = 64
DIL_HEADS = 4
N_GROUPS = 4
EXPERTS_PER_GROUP = 8
N_EXPERTS = N_GROUPS * EXPERTS_PER_GROUP

LANES = 128
SUBLANES = 8
MOE_BLK = 256
VMEM_LIMIT = 56 * 1024 * 1024

C_QA, C_KA, C_QD, C_KD = 0, 512, 1024, 1792
N_ROT = 2560
C_VA, C_GATE, C_QN, C_KN, C_VN, C_VD = 2560, 3072, 6144, 6400, 6656, 6912
NC = 7680
IN_PERM = ((0, 1024), (2304, 3840), (1024, 1536), (4608, 7680), (1536, 2304), (3840, 4608))


def _nt_dot(a, b):
    return lax.dot_general(a, b, (((1,), (1,)), ((), ())), preferred_element_type=F32)


def _stack_heads(q, nh):
    lane = lax.broadcasted_iota(I32, q.shape, 1)
    masks = [(lane >= HEAD_DIM * h) & (lane < HEAD_DIM * (h + 1)) for h in range(nh)]
    return jnp.concatenate([jnp.where(hm, q, jnp.zeros_like(q)) for hm in masks], axis=0), masks


def _unstack_heads(x, masks):
    m = x.shape[0] // len(masks)
    out = x[0:m]
    for h in range(1, len(masks)):
        out = jnp.where(masks[h], x[h * m:(h + 1) * m], out)
    return out


def _params(*sem):
    return pltpu.CompilerParams(dimension_semantics=sem, vmem_limit_bytes=VMEM_LIMIT)


def _ada_kernel(c_ref, w_ref, b_ref, o_ref):
    c = c_ref[...]
    ca = c / (1.0 + jnp.exp(-c))
    o_ref[...] = jnp.dot(ca, w_ref[...], preferred_element_type=F32, precision=HIGHEST) + b_ref[...]


def _ada(c, ada_w, ada_b):
    depth, d, w6 = ada_w.shape
    bsz = c.shape[0]
    rows = -(-bsz // SUBLANES) * SUBLANES
    cp = jnp.zeros((rows, d), F32).at[:bsz].set(c)
    tn = w6 // 4
    out = pl.pallas_call(
        _ada_kernel,
        grid=(depth, w6 // tn),
        in_specs=[pl.BlockSpec((rows, d), lambda l, j: (0, 0)),
                  pl.BlockSpec((None, d, tn), lambda l, j: (l, 0, j)),
                  pl.BlockSpec((None, 1, tn), lambda l, j: (l, 0, j))],
        out_specs=pl.BlockSpec((None, rows, tn), lambda l, j: (l, 0, j)),
        out_shape=jax.ShapeDtypeStruct((depth, rows, w6), F32),
        compiler_params=_params("arbitrary", "arbitrary"),
        name="ada_mod",
    )(cp, ada_w, ada_b.reshape(depth, 1, w6))
    return out[:, :bsz]


def _inproj_kernel(x_ref, g_ref, sc_ref, sh_ref, cos_ref, sin_ref, cs_ref, w_hbm, z_ref, h_scr, w_scr, sem, *, tn):
    @pl.when(pl.program_id(0) == 0)
    def _():
        cp = pltpu.make_async_copy(w_hbm, w_scr, sem)
        cp.start()
        cp.wait()

    x = x_ref[...]
    ms = jnp.mean(x * x, axis=-1, keepdims=True)
    y = x * lax.rsqrt(ms + NORM_EPS) * g_ref[...]
    h_scr[...] = (y * (1.0 + sc_ref[...]) + sh_ref[...]).astype(BF16)

    half = HEAD_DIM // 2
    lane = lax.broadcasted_iota(I32, (x.shape[0], LANES), 1)
    lo = (lane % HEAD_DIM) < half
    for j in range(w_scr.shape[1] // tn):
        cols = slice(j * tn, (j + 1) * tn)
        acc = jnp.dot(h_scr[...], w_scr[:, cols], preferred_element_type=F32) * cs_ref[:, cols]
        if j < N_ROT // tn:
            cos = cos_ref[...]
            sin = sin_ref[...]
            for cc in range(tn // LANES):
                a = acc[:, cc * LANES:(cc + 1) * LANES]
                rot = jnp.where(lo, pltpu.roll(a, LANES - half, 1), pltpu.roll(a, half, 1))
                c0 = j * tn + cc * LANES
                z_ref[:, c0:c0 + LANES] = (a * cos + rot * sin).astype(BF16)
        else:
            z_ref[:, cols] = acc.astype(BF16)


def _inproj(x2, g, scale, shift, cos_t, sin_t, colscale, w_bf, seq, tm=512, tn=512):
    n, d = x2.shape
    nc = w_bf.shape[1]
    assert N_ROT % tn == 0 and nc % tn == 0
    spb = seq // tm
    return pl.pallas_call(
        functools.partial(_inproj_kernel, tn=tn),
        grid=(n // tm,),
        in_specs=[pl.BlockSpec((tm, d), lambda i: (i, 0)),
                  pl.BlockSpec((1, d), lambda i: (0, 0)),
                  pl.BlockSpec((None, 1, d), lambda i: (i // spb, 0, 0)),
                  pl.BlockSpec((None, 1, d), lambda i: (i // spb, 0, 0)),
                  pl.BlockSpec((tm, LANES), lambda i: (i % spb, 0)),
                  pl.BlockSpec((tm, LANES), lambda i: (i % spb, 0)),
                  pl.BlockSpec((1, nc), lambda i: (0, 0)),
                  pl.BlockSpec(memory_space=pl.ANY)],
        out_specs=pl.BlockSpec((tm, nc), lambda i: (i, 0)),
        out_shape=jax.ShapeDtypeStruct((n, nc), BF16),
        scratch_shapes=[pltpu.VMEM((tm, d), BF16), pltpu.VMEM((d, nc), BF16), pltpu.SemaphoreType.DMA],
        compiler_params=_params("arbitrary"),
        name="inproj",
    )(x2, g, scale, shift, cos_t, sin_t, colscale, w_bf)


def _diff_attn_kernel(lam_ref, g_ref, q_ref, k_ref, v_ref, o_ref, *, lam_init, chunk):
    lp = lam_ref[...]
    lam = (jnp.exp(jnp.sum(lp[0:1] * lp[1:2], axis=-1, keepdims=True))
           - jnp.exp(jnp.sum(lp[2:3] * lp[3:4], axis=-1, keepdims=True)) + lam_init)
    k = k_ref[...]
    v = v_ref[...]
    lane = lax.broadcasted_iota(I32, (chunk, 2 * HEAD_DIM), 1)
    chains = [(c, m) for c in range(q_ref.shape[0] // chunk) for m in range(2)]

    def scores(c, m):
        q = q_ref[c * chunk:(c + 1) * chunk, :]
        qm = jnp.where((lane >= HEAD_DIM * m) & (lane < HEAD_DIM * (m + 1)), q, jnp.zeros_like(q))
        return _nt_dot(qm, k)

    s_next = scores(*chains[0])
    outs = {}
    for i, (c, m) in enumerate(chains):
        s = s_next
        if i + 1 < len(chains):
            s_next = scores(*chains[i + 1])
        mx = jnp.max(s, axis=-1, keepdims=True)
        p = jnp.exp(s - mx)
        l = jnp.sum(p, axis=-1, keepdims=True)
        outs[m] = jnp.dot(p.astype(BF16), v, preferred_element_type=F32) / l
        if m == 1:
            o = outs[0] - lam * outs[1]
            ms = jnp.mean(o * o, axis=-1, keepdims=True)
            o = o * lax.rsqrt(ms + NORM_EPS) * g_ref[...] * (1.0 - lam_init)
            o_ref[c * chunk:(c + 1) * chunk, :] = o.astype(BF16)


def _diff_attn(z, lam_p, subln_g, lam_init, bsz, seq, tq=512, chunk=256):
    n = z.shape[0]
    hw = 2 * HEAD_DIM
    nq = seq // tq
    return pl.pallas_call(
        functools.partial(_diff_attn_kernel, lam_init=lam_init, chunk=min(chunk, tq)),
        grid=(bsz, DA_HEADS, nq),
        in_specs=[pl.BlockSpec((4, HEAD_DIM), lambda b, h, i: (0, 0)),
                  pl.BlockSpec((1, hw), lambda b, h, i: (0, 0)),
                  pl.BlockSpec((tq, hw), lambda b, h, i: (b * nq + i, C_QA // hw + h)),
                  pl.BlockSpec((seq, hw), lambda b, h, i: (b, C_KA // hw + h)),
                  pl.BlockSpec((seq, hw), lambda b, h, i: (b, C_VA // hw + h))],
        out_specs=pl.BlockSpec((tq, hw), lambda b, h, i: (b * nq + i, h)),
        out_shape=jax.ShapeDtypeStruct((n, DA_HEADS * hw), BF16),
        compiler_params=_params("arbitrary", "arbitrary", "arbitrary"),
        name="diff_attn",
    )(lam_p, subln_g, z, z, z)


def _na_kernel(q_ref, k_ref, v_ref, bias_ref, o_ref, *, rows):
    win = NA_WIN_ROWS * GRID_W

    def body(r, carry):
        kr0 = jnp.clip(r - NA_WIN_ROWS // 2, 0, rows - NA_WIN_ROWS)
        pat = kr0 - r + (NA_WIN_ROWS - 1)
        q4, masks = _stack_heads(q_ref[pl.ds(pl.multiple_of(r * GRID_W, GRID_W), GRID_W), :], NA_HEADS)
        kw = k_ref[pl.ds(pl.multiple_of(kr0 * GRID_W, GRID_W), win), :]
        vw = v_ref[pl.ds(pl.multiple_of(kr0 * GRID_W, GRID_W), win), :]
        s = _nt_dot(q4, kw) + bias_ref[pat]
        mx = jnp.max(s, axis=-1, keepdims=True)
        p = jnp.exp(s - mx)
        l = jnp.sum(p, axis=-1, keepdims=True)
        o4 = jnp.dot(p.astype(BF16), vw, preferred_element_type=F32) / l
        o_ref[pl.ds(pl.multiple_of(r * GRID_W, GRID_W), GRID_W), :] = _unstack_heads(o4, masks).astype(BF16)
        return carry

    lax.fori_loop(0, rows, body, 0, unroll=2)


def _na_bias_table(rpb):
    c = jnp.arange(GRID_W)[:, None]
    kc = jnp.arange(GRID_W)[None, :]
    kc0 = jnp.clip(c - NA_WIN_COLS // 2, 0, GRID_W - NA_WIN_COLS)
    valid = (kc >= kc0) & (kc < kc0 + NA_WIN_COLS)
    off = GRID_W - NA_WIN_COLS
    padded = jnp.pad(rpb.astype(F32), ((0, 0), (0, 0), (off, off)))
    cols = jnp.stack([padded[:, :, GRID_W - 1 - q:2 * GRID_W - 1 - q] for q in range(GRID_W)], axis=2)
    cols = jnp.where(valid[None, None], cols, NEG_INF)
    tab = jnp.stack([cols[:, p:p + NA_WIN_ROWS] for p in range(NA_WIN_ROWS)], axis=0)
    tab = tab.transpose(0, 1, 3, 2, 4)
    return tab.reshape(NA_WIN_ROWS, NA_HEADS * GRID_W, NA_WIN_ROWS * GRID_W)


def _na_attn(z, bias_tab, bsz, seq):
    n = z.shape[0]
    width = NA_HEADS * HEAD_DIM
    rows = seq // GRID_W
    assert rows >= NA_WIN_ROWS
    return pl.pallas_call(
        functools.partial(_na_kernel, rows=rows),
        grid=(bsz,),
        in_specs=[pl.BlockSpec((seq, width), lambda b: (b, C_QN // width)),
                  pl.BlockSpec((seq, width), lambda b: (b, C_KN // width)),
                  pl.BlockSpec((seq, width), lambda b: (b, C_VN // width)),
                  pl.BlockSpec(bias_tab.shape, lambda b: (0, 0, 0))],
        out_specs=pl.BlockSpec((seq, width), lambda b: (b, 0)),
        out_shape=jax.ShapeDtypeStruct((n, width), BF16),
        compiler_params=_params("arbitrary"),
        name="na_attn",
    )(z, z, z, bias_tab)


def _dil_kernel(q_ref, k_ref, v_ref, o_ref, lse_ref, *scratch, dil, seg, win, tq):
    width = DIL_HEADS * HEAD_DIM
    nlc = width // LANES
    qi = pl.program_id(1)
    if dil > 1:
        kd_scr, vd_scr, stage_scr, ostage_scr, lstage_scr = scratch

        @pl.when(qi == 0)
        def _():
            for src, dst in ((k_ref, kd_scr), (v_ref, vd_scr)):
                for lc in range(nlc):
                    stage_scr[lc] = src[:, lc * LANES:(lc + 1) * LANES].astype(F32)
                for r in range(dil):
                    for lc in range(nlc):
                        dst[r * seg:(r + 1) * seg, lc * LANES:(lc + 1) * LANES] = (
                            stage_scr[lc, pl.ds(r, seg, stride=dil), :].astype(BF16))

        for lc in range(nlc):
            stage_scr[lc, 0:tq * dil, :] = q_ref[:, lc * LANES:(lc + 1) * LANES].astype(F32)
        k_src, v_src = kd_scr, vd_scr
    else:
        k_src, v_src = k_ref, v_ref
    a0 = qi * tq
    w0 = jnp.clip(a0 - DIL_SIDE, 0, seg - win)
    aq = a0 + lax.broadcasted_iota(I32, (DIL_HEADS * tq, win), 0) % tq
    ak = w0 + lax.broadcasted_iota(I32, (DIL_HEADS * tq, win), 1)
    valid = jnp.abs(aq - ak) <= DIL_SIDE

    def residue(r):
        if dil > 1:
            q = jnp.concatenate([stage_scr[lc, pl.ds(r, tq, stride=dil), :] for lc in range(nlc)], axis=1).astype(BF16)
        else:
            q = q_ref[...]
        start = pl.multiple_of(r * seg + w0, DIL_SIDE)
        kw = k_src[pl.ds(start, win), :]
        vw = v_src[pl.ds(start, win), :]
        q4, masks = _stack_heads(q, DIL_HEADS)
        s = jnp.where(valid, _nt_dot(q4, kw), NEG_INF)
        mx = jnp.max(s, axis=-1, keepdims=True)
        p = jnp.exp(s - mx)
        l = jnp.sum(p, axis=-1, keepdims=True)
        o = _unstack_heads(jnp.dot(p.astype(BF16), vw, preferred_element_type=F32) / l, masks)
        lse = _unstack_heads(jnp.broadcast_to(mx + jnp.log(l), (DIL_HEADS * tq, width)), masks)
        if dil > 1:
            for lc in range(nlc):
                ostage_scr[lc, pl.ds(r, tq, stride=dil), :] = o[:, lc * LANES:(lc + 1) * LANES]
                lstage_scr[lc, pl.ds(r, tq, stride=dil), :] = lse[:, lc * LANES:(lc + 1) * LANES]
        else:
            o_ref[...] = o
            lse_ref[...] = lse

    if dil == 1:
        residue(0)
    else:
        def body(r, carry):
            residue(r)
            return carry

        lax.fori_loop(0, dil, body, 0, unroll=2)
        for lc in range(nlc):
            o_ref[:, lc * LANES:(lc + 1) * LANES] = ostage_scr[lc]
            lse_ref[:, lc * LANES:(lc + 1) * LANES] = lstage_scr[lc]


def _dil_attn(z, g, bsz, seq):
    dil = DIL_DILATIONS[g]
    n = z.shape[0]
    width = DIL_HEADS * HEAD_DIM
    seg = seq // dil
    tq = min(128, seg)
    win = min(2 * DIL_SIDE + tq, seg)
    nq = seg // tq
    cq, ck, cv = (C_QD // width + g, C_KD // width + g, C_VD // width + g)
    scratch = []
    if dil > 1:
        nlc = width // LANES
        scratch = [pltpu.VMEM((seq, width), BF16), pltpu.VMEM((seq, width), BF16),
                   pltpu.VMEM((nlc, seq, LANES), F32),
                   pltpu.VMEM((nlc, tq * dil, LANES), F32), pltpu.VMEM((nlc, tq * dil, LANES), F32)]
    return pl.pallas_call(
        functools.partial(_dil_kernel, dil=dil, seg=seg, win=win, tq=tq),
        grid=(bsz, nq),
        in_specs=[pl.BlockSpec((tq * dil, width), lambda b, i: (b * nq + i, cq)),
                  pl.BlockSpec((seq, width), lambda b, i: (b, ck)),
                  pl.BlockSpec((seq, width), lambda b, i: (b, cv))],
        out_specs=[pl.BlockSpec((tq * dil, width), lambda b, i: (b * nq + i, 0)),
                   pl.BlockSpec((tq * dil, width), lambda b, i: (b * nq + i, 0))],
        out_shape=[jax.ShapeDtypeStruct((n, width), F32), jax.ShapeDtypeStruct((n, width), F32)],
        scratch_shapes=scratch,
        compiler_params=_params("arbitrary", "arbitrary"),
        name=f"dil_attn_g{g}",
    )(z, z, z)


def _mix_out_kernel(x_ref, oa_ref, on_ref, od0_ref, od1_ref, od2_ref, l0_ref, l1_ref, l2_ref,
                    ga_ref, gn_ref, gd_ref, wpa_ref, wpn_ref, wpd_ref, wout_ref,
                    gm_ref, fg_ref, fsc_ref, fsh_ref, wr_ref, br_ref,
                    xo_ref, h2_ref, lg_ref):
    tm = x_ref.shape[0]
    l0, l1, l2 = l0_ref[...], l1_ref[...], l2_ref[...]
    mx = jnp.maximum(jnp.maximum(l0, l1), l2)
    e0, e1, e2 = jnp.exp(l0 - mx), jnp.exp(l1 - mx), jnp.exp(l2 - mx)
    den = e0 + e1 + e2
    od = (e0 / den) * od0_ref[...] + (e1 / den) * od1_ref[...] + (e2 / den) * od2_ref[...]

    def sig(ref):
        g = ref[...].astype(F32)
        return 1.0 / (1.0 + jnp.exp(-g))

    merged = (sig(ga_ref) * jnp.dot(oa_ref[...], wpa_ref[...], preferred_element_type=F32)
              + sig(gn_ref) * jnp.dot(on_ref[...], wpn_ref[...], preferred_element_type=F32)
              + sig(gd_ref) * jnp.dot(od.astype(BF16), wpd_ref[...], preferred_element_type=F32))
    y = jnp.dot(merged.astype(BF16), wout_ref[...], preferred_element_type=F32)
    xn = x_ref[...] + gm_ref[...] * y
    xo_ref[...] = xn
    ms = jnp.mean(xn * xn, axis=-1, keepdims=True)
    h2 = xn * lax.rsqrt(ms + NORM_EPS) * fg_ref[...]
    h2 = h2 * (1.0 + fsc_ref[...]) + fsh_ref[...]
    lg_ref[...] = lax.dot_general(wr_ref[...], h2, (((1,), (1,)), ((), ())), preferred_element_type=F32,
                                  precision=HIGHEST) + br_ref[...]
    for c in range(h2.shape[1] // LANES):
        h2_ref[pl.ds(c, tm, stride=SUBLANES), :] = h2[:, c * LANES:(c + 1) * LANES]


def _mix_out(x2, z, o_a, o_n, o_d, lse_d, wpa, wpn, wpd, wout, gate_m, fg, fsc, fsh, wr, br, seq, tm=256):
    n, d = x2.shape
    assert d // LANES == SUBLANES and C_GATE % d == 0
    spb = seq // tm
    gcol = C_GATE // d
    wd = DIL_HEADS * HEAD_DIM
    row = lambda w: pl.BlockSpec((tm, w), lambda i: (i, 0))
    full = lambda a: pl.BlockSpec(a.shape, lambda i: (0,) * a.ndim)
    perb = pl.BlockSpec((None, 1, d), lambda i: (i // spb, 0, 0))
    return pl.pallas_call(
        _mix_out_kernel,
        grid=(n // tm,),
        in_specs=[row(d), row(o_a.shape[1]), row(o_n.shape[1]), row(wd), row(wd), row(wd), row(wd), row(wd), row(wd),
                  pl.BlockSpec((tm, d), lambda i: (i, gcol)),
                  pl.BlockSpec((tm, d), lambda i: (i, gcol + 1)),
                  pl.BlockSpec((tm, d), lambda i: (i, gcol + 2)),
                  full(wpa), full(wpn), full(wpd), full(wout),
                  perb, full(fg), perb, perb, full(wr), full(br)],
        out_specs=[row(d), pl.BlockSpec((tm * SUBLANES, LANES), lambda i: (i, 0)),
                   pl.BlockSpec((wr.shape[0], tm), lambda i: (0, i))],
        out_shape=[jax.ShapeDtypeStruct((n, d), F32),
                   jax.ShapeDtypeStruct((n * SUBLANES, LANES), F32),
                   jax.ShapeDtypeStruct((wr.shape[0], n), F32)],
        compiler_params=_params("arbitrary"),
        name="mix_out",
    )(x2, o_a, o_n, o_d[0], o_d[1], o_d[2], lse_d[0], lse_d[1], lse_d[2], z, z, z,
      wpa, wpn, wpd, wout, gate_m, fg, fsc, fsh, wr, br)


def _route_kernel(lg_ref, e_ref, w_ref, cnt_ref):
    i = pl.program_id(0)
    le = lg_ref[0:N_EXPERTS, :]
    gl = lg_ref[N_EXPERTS:N_EXPERTS + N_GROUPS, :]
    tb = le.shape[1]
    gmax = jnp.max(gl, axis=0, keepdims=True)
    grow = lax.broadcasted_iota(I32, gl.shape, 0)
    gidx = jnp.min(jnp.where(gl == gmax, grow, N_GROUPS), axis=0, keepdims=True)
    pg_top = 1.0 / jnp.sum(jnp.exp(gl - gmax), axis=0, keepdims=True)
    erow = lax.broadcasted_iota(I32, le.shape, 0)
    lm = jnp.where((erow // EXPERTS_PER_GROUP) == gidx, le, -jnp.inf)
    m1 = jnp.max(lm, axis=0, keepdims=True)
    i1 = jnp.min(jnp.where(lm == m1, erow, N_EXPERTS), axis=0, keepdims=True)
    lm2 = jnp.where(erow == i1, -jnp.inf, lm)
    m2 = jnp.max(lm2, axis=0, keepdims=True)
    i2 = jnp.min(jnp.where(lm2 == m2, erow, N_EXPERTS), axis=0, keepdims=True)
    t = jnp.exp(m2 - m1)
    e_ref[0:1, :] = i1
    e_ref[1:2, :] = i2
    w_ref[0:1, :] = pg_top / (1.0 + t)
    w_ref[1:2, :] = pg_top * t / (1.0 + t)
    oh = (erow == i1).astype(F32) + (erow == i2).astype(F32)
    cnt = jnp.sum(oh, axis=1, keepdims=True)

    @pl.when(i == 0)
    def _():
        cnt_ref[...] = jnp.zeros_like(cnt_ref)

    cnt_ref[...] += jnp.broadcast_to(cnt, cnt_ref.shape)


def _route(lgt, tb=1024):
    rows, n = lgt.shape
    tb = min(tb, n)
    return pl.pallas_call(
        _route_kernel,
        grid=(n // tb,),
        in_specs=[pl.BlockSpec((rows, tb), lambda i: (0, i))],
        out_specs=[pl.BlockSpec((2, tb), lambda i: (0, i)),
                   pl.BlockSpec((2, tb), lambda i: (0, i)),
                   pl.BlockSpec((N_EXPERTS, LANES), lambda i: (0, 0))],
        out_shape=[jax.ShapeDtypeStruct((2, n), I32),
                   jax.ShapeDtypeStruct((2, n), F32),
                   jax.ShapeDtypeStruct((N_EXPERTS, LANES), F32)],
        compiler_params=_params("arbitrary"),
        name="route",
    )(lgt)


def _rank_kernel(e_ref, cnt_ref, dest_ref, be_ref, nu_ref, zs_ref, carry_scr, base_scr, *, blk):
    i = pl.program_id(0)

    @pl.when(i == 0)
    def _():
        cnt = cnt_ref[...]
        padded = jnp.floor((cnt + (blk - 1)) / blk) * blk
        r = lax.broadcasted_iota(I32, cnt.shape, 0)
        c = lax.broadcasted_iota(I32, cnt.shape, 1)
        prow = jnp.sum(jnp.where(r == c, padded, 0.0), axis=0, keepdims=True)
        pad_end = jnp.sum(jnp.where(c <= r, prow, 0.0), axis=1, keepdims=True)
        base = jnp.broadcast_to(pad_end, cnt.shape) - padded
        base_scr[...] = base
        carry_scr[...] = jnp.zeros_like(carry_scr)
        zs_ref[...] = jnp.sum(jnp.where(r == c, base + cnt, 0.0) + jnp.where(r + N_EXPERTS == c, padded - cnt, 0.0),
                              axis=0, keepdims=True).astype(I32)
        jb = lax.broadcasted_iota(I32, (N_EXPERTS, be_ref.shape[1]), 1).astype(F32) * blk
        be = jnp.sum((pad_end <= jb).astype(I32), axis=0, keepdims=True)
        be_ref[...] = jnp.minimum(be, N_EXPERTS - 1)
        nu_ref[...] = jnp.broadcast_to(jnp.max(pad_end, axis=0, keepdims=True) / blk, nu_ref.shape).astype(I32)

    e = e_ref[...]
    tb = e.shape[1]
    erow = lax.broadcasted_iota(I32, (N_EXPERTS, tb), 0)
    oh0 = erow == e[0:1]
    oh1 = erow == e[1:2]
    both = jnp.where(oh0 | oh1, 1.0, 0.0)
    upper = jnp.where(lax.broadcasted_iota(I32, (tb, tb), 0) < lax.broadcasted_iota(I32, (tb, tb), 1), 1.0, 0.0)
    prefix = jnp.dot(both.astype(BF16), upper.astype(BF16), preferred_element_type=F32)
    tot = prefix + base_scr[:, 0:1] + carry_scr[:, 0:1]
    dest_ref[0:1, :] = jnp.sum(jnp.where(oh0, tot, 0.0), axis=0, keepdims=True).astype(I32)
    dest_ref[1:2, :] = jnp.sum(jnp.where(oh1, tot, 0.0), axis=0, keepdims=True).astype(I32)
    carry_scr[...] += jnp.broadcast_to(jnp.sum(both, axis=1, keepdims=True), carry_scr.shape)


def _rank(eidx, cnt, n_blocks, blk, tb=512):
    n = eidx.shape[1]
    tb = min(tb, n)
    nbp = -(-n_blocks // LANES) * LANES
    return pl.pallas_call(
        functools.partial(_rank_kernel, blk=blk),
        grid=(n // tb,),
        in_specs=[pl.BlockSpec((2, tb), lambda i: (0, i)),
                  pl.BlockSpec(cnt.shape, lambda i: (0, 0))],
        out_specs=[pl.BlockSpec((2, tb), lambda i: (0, i)),
                   pl.BlockSpec((1, nbp), lambda i: (0, 0)),
                   pl.BlockSpec((1, LANES), lambda i: (0, 0)),
                   pl.BlockSpec((1, LANES), lambda i: (0, 0))],
        out_shape=[jax.ShapeDtypeStruct((2, n), I32),
                   jax.ShapeDtypeStruct((1, nbp), I32),
                   jax.ShapeDtypeStruct((1, LANES), I32),
                   jax.ShapeDtypeStruct((1, LANES), I32)],
        scratch_shapes=[pltpu.VMEM((N_EXPERTS, LANES), F32), pltpu.VMEM((N_EXPERTS, LANES), F32)],
        compiler_params=_params("arbitrary"),
        name="rank",
    )(eidx, cnt)


def _tile_copy(src, src_row, dst, dst_row, sem):
    return pltpu.make_async_copy(src.at[pl.ds(pl.multiple_of(src_row * SUBLANES, SUBLANES), SUBLANES), :],
                                 dst.at[pl.ds(pl.multiple_of(dst_row * SUBLANES, SUBLANES), SUBLANES), :], sem)


def _dispatch_kernel(zs_ref, nu_ref, dest_ref, h_ref, xs_hbm, zero_scr, sem, zsem, *, tb, blk, total_blocks):
    def zero_fill(slot, nslots):
        start = pl.multiple_of(slot * SUBLANES, SUBLANES)
        return pltpu.make_async_copy(zero_scr.at[pl.ds(0, nslots * SUBLANES), :],
                                     xs_hbm.at[pl.ds(start, nslots * SUBLANES), :], zsem)

    def pad_fills(act):
        for e in range(N_EXPERTS):
            off = zs_ref[e]
            count = zs_ref[N_EXPERTS + e]
            for b in range(blk.bit_length() - 1):
                bit = (count >> b) & 1

                @pl.when(bit == 1)
                def _(off=off, b=b):
                    act(zero_fill(off, 1 << b))

                off = off + bit * (1 << b)

    @pl.when(pl.program_id(0) == 0)
    def _():
        zero_scr[...] = jnp.zeros_like(zero_scr)
        pad_fills(lambda cp: cp.start())

        def tail_start(j, carry):
            zero_fill(j * blk, blk).start()
            return carry

        def tail_wait(j, carry):
            zero_fill(j * blk, blk).wait()
            return carry

        lax.fori_loop(nu_ref[0], total_blocks, tail_start, 0)
        pad_fills(lambda cp: cp.wait())
        lax.fori_loop(nu_ref[0], total_blocks, tail_wait, 0)

    def issue(t, carry):
        for k in range(2):
            _tile_copy(h_ref, t, xs_hbm, dest_ref[k, t], sem).start()
        return carry

    lax.fori_loop(0, tb, issue, 0)
    for k in range(2):
        pltpu.make_async_copy(h_ref, xs_hbm.at[pl.ds(0, tb * SUBLANES), :], sem).wait()


def _dispatch(zstart, n_used, dest, h2t, n_blocks, blk, tb=512):
    n = dest.shape[1]
    tb = min(tb, n)
    grid_spec = pltpu.PrefetchScalarGridSpec(
        num_scalar_prefetch=2,
        grid=(n // tb,),
        in_specs=[pl.BlockSpec((2, tb), lambda i, zs, nu: (0, i), memory_space=pltpu.SMEM),
                  pl.BlockSpec((tb * SUBLANES, LANES), lambda i, zs, nu: (i, 0))],
        out_specs=pl.BlockSpec(memory_space=pl.ANY),
        scratch_shapes=[pltpu.VMEM((blk * SUBLANES, LANES), F32), pltpu.SemaphoreType.DMA, pltpu.SemaphoreType.DMA],
    )
    total_blocks = n_blocks
    return pl.pallas_call(
        functools.partial(_dispatch_kernel, tb=tb, blk=blk, total_blocks=total_blocks),
        grid_spec=grid_spec,
        out_shape=jax.ShapeDtypeStruct((total_blocks * blk * SUBLANES, LANES), F32),
        compiler_params=pltpu.CompilerParams(dimension_semantics=("arbitrary",), has_side_effects=True),
        name="dispatch",
    )(zstart, n_used, dest, h2t)


def _expert_kernel(be_ref, first_ref, nu_ref, xs_ref, w1_ref, w3_ref, w2_ref, y_ref, xb_scr, w1_scr, w3_scr, w2_scr, *, blk):
    del be_ref
    j = pl.program_id(0)
    nchunk = xb_scr.shape[1] // LANES

    @pl.when(j < nu_ref[0])
    def _():
        @pl.when(first_ref[j] == 1)
        def _():
            w1_scr[...] = w1_ref[...].astype(BF16)
            w3_scr[...] = w3_ref[...].astype(BF16)
            w2_scr[...] = w2_ref[...].astype(BF16)

        for c in range(nchunk):
            xb_scr[:, c * LANES:(c + 1) * LANES] = xs_ref[pl.ds(c, blk, stride=SUBLANES), :].astype(BF16)
        xb = xb_scr[...]
        a = jnp.dot(xb, w1_scr[...], preferred_element_type=F32)
        b = jnp.dot(xb, w3_scr[...], preferred_element_type=F32)
        hid = (a / (1.0 + jnp.exp(-a))) * b
        y = jnp.dot(hid.astype(BF16), w2_scr[...], preferred_element_type=F32)
        for c in range(nchunk):
            y_ref[pl.ds(c, blk, stride=SUBLANES), :] = y[:, c * LANES:(c + 1) * LANES]

    @pl.when(j >= nu_ref[0])
    def _():
        y_ref[...] = jnp.zeros_like(y_ref)


def _experts(block_e, first, n_used, xs, w1, w3, w2, layer, n_blocks, blk):
    d, hid = w1.shape[2], w1.shape[3]
    used = lambda j, nu: jnp.minimum(j, nu[0] - 1)
    grid_spec = pltpu.PrefetchScalarGridSpec(
        num_scalar_prefetch=3,
        grid=(n_blocks,),
        in_specs=[pl.BlockSpec((blk * SUBLANES, LANES), lambda j, be, fi, nu: (used(j, nu), 0)),
                  pl.BlockSpec((None, None, d, hid), lambda j, be, fi, nu: (layer, be[j], 0, 0)),
                  pl.BlockSpec((None, None, d, hid), lambda j, be, fi, nu: (layer, be[j], 0, 0)),
                  pl.BlockSpec((None, None, hid, d), lambda j, be, fi, nu: (layer, be[j], 0, 0))],
        out_specs=pl.BlockSpec((blk * SUBLANES, LANES), lambda j, be, fi, nu: (j, 0)),
        scratch_shapes=[pltpu.VMEM((blk, d), BF16), pltpu.VMEM((d, hid), BF16), pltpu.VMEM((d, hid), BF16),
                        pltpu.VMEM((hid, d), BF16)],
    )
    return pl.pallas_call(
        functools.partial(_expert_kernel, blk=blk),
        grid_spec=grid_spec,
        out_shape=jax.ShapeDtypeStruct((n_blocks * blk * SUBLANES, LANES), F32),
        compiler_params=_params("arbitrary"),
        name="experts",
    )(block_e, first, n_used, xs, w1, w3, w2)


def _combine_kernel(dcur_ref, dnext_ref, x_ref, w_ref, gf_ref, y_hbm, xo_ref, g_scr, sems, *, tb, nsteps):
    i = pl.program_id(0)
    slot = i % 2

    def gather(d_ref, s):
        def issue(t, carry):
            for k in range(2):
                _tile_copy(y_hbm, d_ref[k, t], g_scr.at[s, k], t, sems.at[s]).start()
            return carry

        lax.fori_loop(0, tb, issue, 0)

    @pl.when(i == 0)
    def _():
        gather(dcur_ref, 0)

    @pl.when(i + 1 < nsteps)
    def _():
        gather(dnext_ref, 1 - slot)

    for k in range(2):
        pltpu.make_async_copy(y_hbm.at[pl.ds(0, tb * SUBLANES), :], g_scr.at[slot, k], sems.at[slot]).wait()
    w0 = w_ref[:, 0:1]
    w1 = w_ref[:, 1:2]
    for c in range(x_ref.shape[1] // LANES):
        sl = slice(c * LANES, (c + 1) * LANES)
        yc = (w0 * g_scr[slot, 0, pl.ds(c, tb, stride=SUBLANES), :]
              + w1 * g_scr[slot, 1, pl.ds(c, tb, stride=SUBLANES), :])
        xo_ref[:, sl] = x_ref[:, sl] + gf_ref[:, sl] * yc


def _combine(dest, x2, wts_t, gate_f, y, seq, tb=256):
    n, d = x2.shape
    spb = seq // tb
    nsteps = n // tb
    return pl.pallas_call(
        functools.partial(_combine_kernel, tb=tb, nsteps=nsteps),
        grid=(nsteps,),
        in_specs=[pl.BlockSpec((2, tb), lambda i: (0, i), memory_space=pltpu.SMEM),
                  pl.BlockSpec((2, tb), lambda i: (0, jnp.minimum(i + 1, nsteps - 1)), memory_space=pltpu.SMEM),
                  pl.BlockSpec((tb, d), lambda i: (i, 0)),
                  pl.BlockSpec((tb, 2), lambda i: (i, 0)),
                  pl.BlockSpec((None, 1, d), lambda i: (i // spb, 0, 0)),
                  pl.BlockSpec(memory_space=pl.ANY)],
        out_specs=pl.BlockSpec((tb, d), lambda i: (i, 0)),
        out_shape=jax.ShapeDtypeStruct((n, d), F32),
        scratch_shapes=[pltpu.VMEM((2, 2, tb * SUBLANES, LANES), F32), pltpu.SemaphoreType.DMA((2,))],
        compiler_params=_params("arbitrary"),
        name="combine",
    )(dest, dest, x2, wts_t, gate_f, y)


def _final_norm_kernel(x_ref, g_ref, o_ref):
    x = x_ref[...]
    ms = jnp.mean(x * x, axis=-1, keepdims=True)
    o_ref[...] = x * lax.rsqrt(ms + NORM_EPS) * g_ref[...]


def _final_norm(x2, g, tm=512):
    n, d = x2.shape
    return pl.pallas_call(
        _final_norm_kernel,
        grid=(n // tm,),
        in_specs=[pl.BlockSpec((tm, d), lambda i: (i, 0)), pl.BlockSpec((1, d), lambda i: (0, 0))],
        out_specs=pl.BlockSpec((tm, d), lambda i: (i, 0)),
        out_shape=jax.ShapeDtypeStruct((n, d), F32),
        compiler_params=_params("arbitrary"),
        name="final_norm",
    )(x2, g)


def _rotary_tables(seq, width):
    inv = 1.0 / (ROPE_THETA ** (jnp.arange(0, HEAD_DIM, 2, dtype=F32) / HEAD_DIM))
    ang = jnp.arange(seq, dtype=F32)[:, None] * inv[None, :]
    ang = jnp.concatenate([ang, ang], axis=-1)
    sign = jnp.concatenate([-jnp.ones((HEAD_DIM // 2,), F32), jnp.ones((HEAD_DIM // 2,), F32)])
    reps = width // HEAD_DIM
    return jnp.tile(jnp.cos(ang), (1, reps)), jnp.tile(jnp.sin(ang) * sign[None, :], (1, reps))


def kernel(x, c, ada_w, ada_b, mix_norm_g, ffn_norm_g, w_in, da_lambda, da_subln_g, na_rpb, w_proj_a, w_proj_n, w_proj_d, w_out, router_group_w, router_group_b, router_expert_w, router_expert_b, expert_w1, expert_w3, expert_w2, final_norm_g):
    bsz, seq, d = x.shape
    depth = ada_w.shape[0]
    n = bsz * seq
    tn = 512
    assert w_in.shape[2] == NC

    mods = _ada(c, ada_w, ada_b).reshape(depth, bsz, 6, 1, d)
    cos_t, sin_t = _rotary_tables(seq, LANES)
    scale = HEAD_DIM ** -0.5
    colscale = jnp.ones((1, NC), F32)
    for lo, hi in ((C_QA, C_KA), (C_QD, C_KD), (C_QN, C_KN)):
        colscale = colscale.at[:, lo:hi].set(scale)

    n_blocks = (2 * n) // MOE_BLK + N_EXPERTS
    rpad = SUBLANES - N_GROUPS

    x2 = x.reshape(n, d)
    for l in range(depth):
        lam_init = 0.8 - 0.6 * math.exp(-0.3 * l)
        shift_m, scale_m, gate_m, shift_f, scale_f, gate_f = (mods[l, :, k] for k in range(6))
        w_bf = jnp.concatenate([w_in[l, :, lo:hi] for lo, hi in IN_PERM], axis=1).astype(BF16)
        z = _inproj(x2, mix_norm_g[l].reshape(1, d), scale_m, shift_m, cos_t, sin_t, colscale, w_bf, seq, tn=tn)
        o_a = _diff_attn(z, da_lambda[l], da_subln_g[l].reshape(1, 2 * HEAD_DIM), lam_init, bsz, seq)
        o_n = _na_attn(z, _na_bias_table(na_rpb[l]), bsz, seq)
        dil = [_dil_attn(z, g, bsz, seq) for g in range(len(DIL_DILATIONS))]
        wr = jnp.concatenate([router_expert_w[l], router_group_w[l], jnp.zeros((d, rpad), F32)], axis=1).T
        br = jnp.concatenate([router_expert_b[l], router_group_b[l], jnp.zeros((rpad,), F32)]).reshape(-1, 1)
        x2, h2t, lgt = _mix_out(
            x2, z, o_a, o_n, [o for o, _ in dil], [s for _, s in dil],
            w_proj_a[l].astype(BF16), w_proj_n[l].astype(BF16), w_proj_d[l].astype(BF16), w_out[l].astype(BF16),
            gate_m, ffn_norm_g[l].reshape(1, d), scale_f, shift_f, wr, br, seq)
        eidx, wts, cnt = _route(lgt)
        dest, block_e, n_used, zstart = _rank(eidx, cnt, n_blocks, MOE_BLK)
        block_e = block_e[0, :n_blocks]
        first = jnp.concatenate([jnp.ones((1,), I32), (block_e[1:] != block_e[:-1]).astype(I32)])
        n_used = n_used[0, :1]
        xs = _dispatch(zstart[0, :2 * N_EXPERTS], n_used, dest, h2t, n_blocks, MOE_BLK)
        y = _experts(block_e, first, n_used, xs, expert_w1, expert_w3, expert_w2, l, n_blocks, MOE_BLK)
        x2 = _combine(dest, x2, wts.T, gate_f, y, seq)
    return _final_norm(x2, final_norm_g.reshape(1, d)).reshape(bsz, seq, d)
```

```python
import functools
import math

import jax
import jax.numpy as jnp
from jax import lax
from jax.experimental import pallas as pl
from jax.experimental.pallas import tpu as pltpu

F32 = jnp.float32
BF16 = jnp.bfloat16
I32 = jnp.int32
HIGHEST = lax.Precision.HIGHEST

HEAD_DIM = 64
ROPE_THETA = 10000.0
NORM_EPS = 1e-6
NEG_INF = -1e30

DA_HEADS = 4
GRID_W = 64
NA_HEADS = 4
NA_WIN_ROWS = 8
NA_WIN_COLS = 16
DIL_DILATIONS = (1, 4, 16)
DIL_SIDE = 64
DIL_HEADS = 4
N_GROUPS = 4
EXPERTS_PER_GROUP = 8
N_EXPERTS = N_GROUPS * EXPERTS_PER_GROUP

LANES = 128
SUBLANES = 8
MOE_BLK = 256
VMEM_LIMIT = 56 * 1024 * 1024

C_QA, C_KA, C_QD, C_KD = 0, 512, 1024, 1792
N_ROT = 2560
C_VA, C_GATE, C_QN, C_KN, C_VN, C_VD = 2560, 3072, 6144, 6400, 6656, 6912
NC = 7680
IN_PERM = ((0, 1024), (2304, 3840), (1024, 1536), (4608, 7680), (1536, 2304), (3840, 4608))


def _nt_dot(a, b):
    return lax.dot_general(a, b, (((1,), (1,)), ((), ())), preferred_element_type=F32)


def _stack_heads(q, nh):
    lane = lax.broadcasted_iota(I32, q.shape, 1)
    masks = [(lane >= HEAD_DIM * h) & (lane < HEAD_DIM * (h + 1)) for h in range(nh)]
    return jnp.concatenate([jnp.where(hm, q, jnp.zeros_like(q)) for hm in masks], axis=0), masks


def _unstack_heads(x, masks):
    m = x.shape[0] // len(masks)
    out = x[0:m]
    for h in range(1, len(masks)):
        out = jnp.where(masks[h], x[h * m:(h + 1) * m], out)
    return out


def _params(*sem):
    return pltpu.CompilerParams(dimension_semantics=sem, vmem_limit_bytes=VMEM_LIMIT)


def _ada_kernel(c_ref, w_ref, b_ref, o_ref):
    c = c_ref[...]
    ca = c / (1.0 + jnp.exp(-c))
    o_ref[...] = jnp.dot(ca, w_ref[...], preferred_element_type=F32, precision=HIGHEST) + b_ref[...]


def _ada(c, ada_w, ada_b):
    depth, d, w6 = ada_w.shape
    bsz = c.shape[0]
    rows = -(-bsz // SUBLANES) * SUBLANES
    cp = jnp.zeros((rows, d), F32).at[:bsz].set(c)
    tn = w6 // 4
    out = pl.pallas_call(
        _ada_kernel,
        grid=(depth, w6 // tn),
        in_specs=[pl.BlockSpec((rows, d), lambda l, j: (0, 0)),
                  pl.BlockSpec((None, d, tn), lambda l, j: (l, 0, j)),
                  pl.BlockSpec((None, 1, tn), lambda l, j: (l, 0, j))],
        out_specs=pl.BlockSpec((None, rows, tn), lambda l, j: (l, 0, j)),
        out_shape=jax.ShapeDtypeStruct((depth, rows, w6), F32),
        compiler_params=_params("arbitrary", "arbitrary"),
        name="ada_mod",
    )(cp, ada_w, ada_b.reshape(depth, 1, w6))
    return out[:, :bsz]


def _norm_modulate(x, g_ref, sc_ref, sh_ref):
    ms = jnp.mean(x * x, axis=-1, keepdims=True)
    y = x * lax.rsqrt(ms + NORM_EPS) * g_ref[...]
    return y * (1.0 + sc_ref[...]) + sh_ref[...]


def _project_chunk(j, tn, h_scr, w_scr, cs_ref, cos_ref, sin_ref, z_ref):
    cols = slice(j * tn, (j + 1) * tn)
    acc = jnp.dot(h_scr[...], w_scr[:, cols], preferred_element_type=F32) * cs_ref[:, cols]
    if j < N_ROT // tn:
        half = HEAD_DIM // 2
        lane = lax.broadcasted_iota(I32, (acc.shape[0], LANES), 1)
        lo = (lane % HEAD_DIM) < half
        cos = cos_ref[...]
        sin = sin_ref[...]
        for cc in range(tn // LANES):
            a = acc[:, cc * LANES:(cc + 1) * LANES]
            rot = jnp.where(lo, pltpu.roll(a, LANES - half, 1), pltpu.roll(a, half, 1))
            c0 = j * tn + cc * LANES
            z_ref[:, c0:c0 + LANES] = (a * cos + rot * sin).astype(BF16)
    else:
        z_ref[:, cols] = acc.astype(BF16)


def _inproj_kernel(x_ref, g_ref, sc_ref, sh_ref, cos_ref, sin_ref, cs_ref, w_hbm, z_ref, h_scr, w_scr, sem, *, tn):
    @pl.when(pl.program_id(0) == 0)
    def _():
        cp = pltpu.make_async_copy(w_hbm, w_scr, sem)
        cp.start()
        cp.wait()

    h_scr[...] = _norm_modulate(x_ref[...], g_ref, sc_ref, sh_ref).astype(BF16)
    for j in range(w_scr.shape[1] // tn):
        _project_chunk(j, tn, h_scr, w_scr, cs_ref, cos_ref, sin_ref, z_ref)


def _combine_inproj_kernel(dcur_ref, dnext_ref, x_ref, wt_ref, gf_ref, y_hbm,
                           g_ref, sc_ref, sh_ref, cos_ref, sin_ref, cs_ref, w_hbm,
                           xo_ref, z_ref, h_scr, w_scr, g_scr, wsem, gsem, *, tn, tb, nsteps):
    i = pl.program_id(0)

    @pl.when(i == 0)
    def _():
        cp = pltpu.make_async_copy(w_hbm, w_scr, wsem)
        cp.start()

        def issue(t, carry):
            for k in range(2):
                _tile_copy(y_hbm, dcur_ref[k, t], g_scr.at[k], t, gsem).start()
            return carry

        lax.fori_loop(0, tb, issue, 0)
        cp.wait()

    for k in range(2):
        pltpu.make_async_copy(y_hbm.at[pl.ds(0, tb * SUBLANES), :], g_scr.at[k], gsem).wait()
    w0 = wt_ref[:, 0:1]
    w1 = wt_ref[:, 1:2]
    for c in range(x_ref.shape[1] // LANES):
        sl = slice(c * LANES, (c + 1) * LANES)
        yc = w0 * g_scr[0, pl.ds(c, tb, stride=SUBLANES), :] + w1 * g_scr[1, pl.ds(c, tb, stride=SUBLANES), :]
        xo_ref[:, sl] = x_ref[:, sl] + gf_ref[:, sl] * yc
    h_scr[...] = _norm_modulate(xo_ref[...], g_ref, sc_ref, sh_ref).astype(BF16)

    nchunks = w_scr.shape[1] // tn
    per = -(-tb // nchunks)
    for j in range(nchunks):
        _project_chunk(j, tn, h_scr, w_scr, cs_ref, cos_ref, sin_ref, z_ref)
        for t in range(j * per, min((j + 1) * per, tb)):
            for k in range(2):
                _tile_copy(y_hbm, dnext_ref[k, t], g_scr.at[k], t, gsem).start()

    @pl.when(i == nsteps - 1)
    def _():
        for k in range(2):
            pltpu.make_async_copy(y_hbm.at[pl.ds(0, tb * SUBLANES), :], g_scr.at[k], gsem).wait()


def _combine_inproj(dest, x2, wts_t, gate_f, y, g, scale, shift, cos_t, sin_t, colscale, w_bf, seq, tm=512, tn=512):
    n, d = x2.shape
    nc = w_bf.shape[1]
    assert N_ROT % tn == 0 and nc % tn == 0
    spb = seq // tm
    nsteps = n // tm
    perb = pl.BlockSpec((None, 1, d), lambda i: (i // spb, 0, 0))
    return pl.pallas_call(
        functools.partial(_combine_inproj_kernel, tn=tn, tb=tm, nsteps=nsteps),
        grid=(nsteps,),
        in_specs=[pl.BlockSpec((2, tm), lambda i: (0, i), memory_space=pltpu.SMEM),
                  pl.BlockSpec((2, tm), lambda i: (0, jnp.minimum(i + 1, nsteps - 1)), memory_space=pltpu.SMEM),
                  pl.BlockSpec((tm, d), lambda i: (i, 0)),
                  pl.BlockSpec((tm, 2), lambda i: (i, 0)),
                  perb,
                  pl.BlockSpec(memory_space=pl.ANY),
                  pl.BlockSpec((1, d), lambda i: (0, 0)),
                  perb, perb,
                  pl.BlockSpec((tm, LANES), lambda i: (i % spb, 0)),
                  pl.BlockSpec((tm, LANES), lambda i: (i % spb, 0)),
                  pl.BlockSpec((1, nc), lambda i: (0, 0)),
                  pl.BlockSpec(memory_space=pl.ANY)],
        out_specs=[pl.BlockSpec((tm, d), lambda i: (i, 0)), pl.BlockSpec((tm, nc), lambda i: (i, 0))],
        out_shape=[jax.ShapeDtypeStruct((n, d), F32), jax.ShapeDtypeStruct((n, nc), BF16)],
        scratch_shapes=[pltpu.VMEM((tm, d), BF16), pltpu.VMEM((d, nc), BF16),
                        pltpu.VMEM((2, tm * SUBLANES, LANES), F32),
                        pltpu.SemaphoreType.DMA, pltpu.SemaphoreType.DMA],
        compiler_params=_params("arbitrary"),
        name="combine_inproj",
    )(dest, dest, x2, wts_t, gate_f, y, g, scale, shift, cos_t, sin_t, colscale, w_bf)


def _inproj(x2, g, scale, shift, cos_t, sin_t, colscale, w_bf, seq, tm=512, tn=512):
    n, d = x2.shape
    nc = w_bf.shape[1]
    assert N_ROT % tn == 0 and nc % tn == 0
    spb = seq // tm
    return pl.pallas_call(
        functools.partial(_inproj_kernel, tn=tn),
        grid=(n // tm,),
        in_specs=[pl.BlockSpec((tm, d), lambda i: (i, 0)),
                  pl.BlockSpec((1, d), lambda i: (0, 0)),
                  pl.BlockSpec((None, 1, d), lambda i: (i // spb, 0, 0)),
                  pl.BlockSpec((None, 1, d), lambda i: (i // spb, 0, 0)),
                  pl.BlockSpec((tm, LANES), lambda i: (i % spb, 0)),
                  pl.BlockSpec((tm, LANES), lambda i: (i % spb, 0)),
                  pl.BlockSpec((1, nc), lambda i: (0, 0)),
                  pl.BlockSpec(memory_space=pl.ANY)],
        out_specs=pl.BlockSpec((tm, nc), lambda i: (i, 0)),
        out_shape=jax.ShapeDtypeStruct((n, nc), BF16),
        scratch_shapes=[pltpu.VMEM((tm, d), BF16), pltpu.VMEM((d, nc), BF16), pltpu.SemaphoreType.DMA],
        compiler_params=_params("arbitrary"),
        name="inproj",
    )(x2, g, scale, shift, cos_t, sin_t, colscale, w_bf)


def _diff_attn_kernel(lam_ref, g_ref, q_ref, k_ref, v_ref, o_ref, *, lam_init, chunk):
    lp = lam_ref[...]
    lam = (jnp.exp(jnp.sum(lp[0:1] * lp[1:2], axis=-1, keepdims=True))
           - jnp.exp(jnp.sum(lp[2:3] * lp[3:4], axis=-1, keepdims=True)) + lam_init)
    k = k_ref[...]
    v = v_ref[...]
    lane = lax.broadcasted_iota(I32, (chunk, 2 * HEAD_DIM), 1)
    chains = [(c, m) for c in range(q_ref.shape[0] // chunk) for m in range(2)]

    def scores(c, m):
        q = q_ref[c * chunk:(c + 1) * chunk, :]
        qm = jnp.where((lane >= HEAD_DIM * m) & (lane < HEAD_DIM * (m + 1)), q, jnp.zeros_like(q))
        return _nt_dot(qm, k)

    s_next = scores(*chains[0])
    pb, ls = {}, {}
    for i, (c, m) in enumerate(chains):
        s = s_next
        if i + 1 < len(chains):
            s_next = scores(*chains[i + 1])
        mx = jnp.max(s, axis=-1, keepdims=True)
        p = jnp.exp2(s - mx)
        ls[m] = jnp.sum(p, axis=-1, keepdims=True)
        pb[m] = p.astype(BF16)
        if m == 1:
            ratio = (lam * ls[0] / ls[1]).astype(BF16)
            a = pb[0] - pb[1] * ratio
            o = jnp.dot(a, v, preferred_element_type=F32) / ls[0]
            ms = jnp.mean(o * o, axis=-1, keepdims=True)
            o = o * lax.rsqrt(ms + NORM_EPS) * g_ref[...] * (1.0 - lam_init)
            o_ref[c * chunk:(c + 1) * chunk, :] = o.astype(BF16)


def _diff_attn(z, lam_p, subln_g, lam_init, bsz, seq, tq=512, chunk=256):
    n = z.shape[0]
    hw = 2 * HEAD_DIM
    nq = seq // tq
    return pl.pallas_call(
        functools.partial(_diff_attn_kernel, lam_init=lam_init, chunk=min(chunk, tq)),
        grid=(bsz, DA_HEADS, nq),
        in_specs=[pl.BlockSpec((4, HEAD_DIM), lambda b, h, i: (0, 0)),
                  pl.BlockSpec((1, hw), lambda b, h, i: (0, 0)),
                  pl.BlockSpec((tq, hw), lambda b, h, i: (b * nq + i, C_QA // hw + h)),
                  pl.BlockSpec((seq, hw), lambda b, h, i: (b, C_KA // hw + h)),
                  pl.BlockSpec((seq, hw), lambda b, h, i: (b, C_VA // hw + h))],
        out_specs=pl.BlockSpec((tq, hw), lambda b, h, i: (b * nq + i, h)),
        out_shape=jax.ShapeDtypeStruct((n, DA_HEADS * hw), BF16),
        compiler_params=_params("arbitrary", "arbitrary", "arbitrary"),
        name="diff_attn",
    )(lam_p, subln_g, z, z, z)


def _na_kernel(q_ref, k_ref, v_ref, bias_ref, o_ref, *, rows):
    win = NA_WIN_ROWS * GRID_W

    def body(r, carry):
        kr0 = jnp.clip(r - NA_WIN_ROWS // 2, 0, rows - NA_WIN_ROWS)
        pat = kr0 - r + (NA_WIN_ROWS - 1)
        q4, masks = _stack_heads(q_ref[pl.ds(pl.multiple_of(r * GRID_W, GRID_W), GRID_W), :], NA_HEADS)
        kw = k_ref[pl.ds(pl.multiple_of(kr0 * GRID_W, GRID_W), win), :]
        vw = v_ref[pl.ds(pl.multiple_of(kr0 * GRID_W, GRID_W), win), :]
        s = _nt_dot(q4, kw) + bias_ref[pat]
        mx = jnp.max(s, axis=-1, keepdims=True)
        p = jnp.exp(s - mx)
        l = jnp.sum(p, axis=-1, keepdims=True)
        o4 = jnp.dot(p.astype(BF16), vw, preferred_element_type=F32) / l
        o_ref[pl.ds(pl.multiple_of(r * GRID_W, GRID_W), GRID_W), :] = _unstack_heads(o4, masks).astype(BF16)
        return carry

    lax.fori_loop(0, rows, body, 0, unroll=2)


def _na_bias_table(rpb):
    c = jnp.arange(GRID_W)[:, None]
    kc = jnp.arange(GRID_W)[None, :]
    kc0 = jnp.clip(c - NA_WIN_COLS // 2, 0, GRID_W - NA_WIN_COLS)
    valid = (kc >= kc0) & (kc < kc0 + NA_WIN_COLS)
    off = GRID_W - NA_WIN_COLS
    padded = jnp.pad(rpb.astype(F32), ((0, 0), (0, 0), (off, off)))
    cols = jnp.stack([padded[:, :, GRID_W - 1 - q:2 * GRID_W - 1 - q] for q in range(GRID_W)], axis=2)
    cols = jnp.where(valid[None, None], cols, NEG_INF)
    tab = jnp.stack([cols[:, p:p + NA_WIN_ROWS] for p in range(NA_WIN_ROWS)], axis=0)
    tab = tab.transpose(0, 1, 3, 2, 4)
    return tab.reshape(NA_WIN_ROWS, NA_HEADS * GRID_W, NA_WIN_ROWS * GRID_W)


def _na_attn(z, bias_tab, bsz, seq):
    n = z.shape[0]
    width = NA_HEADS * HEAD_DIM
    rows = seq // GRID_W
    assert rows >= NA_WIN_ROWS
    return pl.pallas_call(
        functools.partial(_na_kernel, rows=rows),
        grid=(bsz,),
        in_specs=[pl.BlockSpec((seq, width), lambda b: (b, C_QN // width)),
                  pl.BlockSpec((seq, width), lambda b: (b, C_KN // width)),
                  pl.BlockSpec((seq, width), lambda b: (b, C_VN // width)),
                  pl.BlockSpec(bias_tab.shape, lambda b: (0, 0, 0))],
        out_specs=pl.BlockSpec((seq, width), lambda b: (b, 0)),
        out_shape=jax.ShapeDtypeStruct((n, width), BF16),
        compiler_params=_params("arbitrary"),
        name="na_attn",
    )(z, z, z, bias_tab)


def _dil_kernel(q_ref, k_ref, v_ref, o_ref, lse_ref, *scratch, dil, seg, win, tq):
    width = DIL_HEADS * HEAD_DIM
    nlc = width // LANES
    qi = pl.program_id(1)
    if dil > 1:
        kd_scr, vd_scr, stage_scr, ostage_scr, lstage_scr = scratch

        @pl.when(qi == 0)
        def _():
            for src, dst in ((k_ref, kd_scr), (v_ref, vd_scr)):
                for lc in range(nlc):
                    stage_scr[lc] = src[:, lc * LANES:(lc + 1) * LANES].astype(F32)
                for r in range(dil):
                    for lc in range(nlc):
                        dst[r * seg:(r + 1) * seg, lc * LANES:(lc + 1) * LANES] = (
                            stage_scr[lc, pl.ds(r, seg, stride=dil), :].astype(BF16))

        for lc in range(nlc):
            stage_scr[lc, 0:tq * dil, :] = q_ref[:, lc * LANES:(lc + 1) * LANES].astype(F32)
        k_src, v_src = kd_scr, vd_scr
    else:
        k_src, v_src = k_ref, v_ref
    a0 = qi * tq
    w0 = jnp.clip(a0 - DIL_SIDE, 0, seg - win)
    aq = a0 + lax.broadcasted_iota(I32, (DIL_HEADS * tq, win), 0) % tq
    ak = w0 + lax.broadcasted_iota(I32, (DIL_HEADS * tq, win), 1)
    valid = jnp.abs(aq - ak) <= DIL_SIDE

    def residue(r):
        if dil > 1:
            q = jnp.concatenate([stage_scr[lc, pl.ds(r, tq, stride=dil), :] for lc in range(nlc)], axis=1).astype(BF16)
        else:
            q = q_ref[...]
        start = pl.multiple_of(r * seg + w0, DIL_SIDE)
        kw = k_src[pl.ds(start, win), :]
        vw = v_src[pl.ds(start, win), :]
        q4, masks = _stack_heads(q, DIL_HEADS)
        s = jnp.where(valid, _nt_dot(q4, kw), NEG_INF)
        mx = jnp.max(s, axis=-1, keepdims=True)
        p = jnp.exp(s - mx)
        l = jnp.sum(p, axis=-1, keepdims=True)
        o = _unstack_heads(jnp.dot(p.astype(BF16), vw, preferred_element_type=F32) / l, masks)
        lse = _unstack_heads(jnp.broadcast_to(mx + jnp.log(l), (DIL_HEADS * tq, width)), masks)
        if dil > 1:
            for lc in range(nlc):
                ostage_scr[lc, pl.ds(r, tq, stride=dil), :] = o[:, lc * LANES:(lc + 1) * LANES]
                lstage_scr[lc, pl.ds(r, tq, stride=dil), :] = lse[:, lc * LANES:(lc + 1) * LANES]
        else:
            o_ref[...] = o
            lse_ref[...] = lse

    if dil == 1:
        residue(0)
    else:
        def body(r, carry):
            residue(r)
            return carry

        lax.fori_loop(0, dil, body, 0, unroll=2)
        for lc in range(nlc):
            o_ref[:, lc * LANES:(lc + 1) * LANES] = ostage_scr[lc]
            lse_ref[:, lc * LANES:(lc + 1) * LANES] = lstage_scr[lc]


def _dil_attn(z, g, bsz, seq):
    dil = DIL_DILATIONS[g]
    n = z.shape[0]
    width = DIL_HEADS * HEAD_DIM
    seg = seq // dil
    tq = min(128, seg)
    win = min(2 * DIL_SIDE + tq, seg)
    nq = seg // tq
    cq, ck, cv = (C_QD // width + g, C_KD // width + g, C_VD // width + g)
    scratch = []
    if dil > 1:
        nlc = width // LANES
        scratch = [pltpu.VMEM((seq, width), BF16), pltpu.VMEM((seq, width), BF16),
                   pltpu.VMEM((nlc, seq, LANES), F32),
                   pltpu.VMEM((nlc, tq * dil, LANES), F32), pltpu.VMEM((nlc, tq * dil, LANES), F32)]
    return pl.pallas_call(
        functools.partial(_dil_kernel, dil=dil, seg=seg, win=win, tq=tq),
        grid=(bsz, nq),
        in_specs=[pl.BlockSpec((tq * dil, width), lambda b, i: (b * nq + i, cq)),
                  pl.BlockSpec((seq, width), lambda b, i: (b, ck)),
                  pl.BlockSpec((seq, width), lambda b, i: (b, cv))],
        out_specs=[pl.BlockSpec((tq * dil, width), lambda b, i: (b * nq + i, 0)),
                   pl.BlockSpec((tq * dil, width), lambda b, i: (b * nq + i, 0))],
        out_shape=[jax.ShapeDtypeStruct((n, width), F32), jax.ShapeDtypeStruct((n, width), F32)],
        scratch_shapes=scratch,
        compiler_params=_params("arbitrary", "arbitrary"),
        name=f"dil_attn_g{g}",
    )(z, z, z)


def _mix_out_kernel(x_ref, oa_ref, on_ref, od0_ref, od1_ref, od2_ref, l0_ref, l1_ref, l2_ref,
                    ga_ref, gn_ref, gd_ref, wpa_ref, wpn_ref, wpd_ref, wout_ref,
                    gm_ref, fg_ref, fsc_ref, fsh_ref, wr_ref, br_ref,
                    xo_ref, h2_ref, lg_ref, *, sub):
    def sig(g):
        return 1.0 / (1.0 + jnp.exp(-g.astype(F32)))

    for r0 in range(0, x_ref.shape[0], sub):
        rows = slice(r0, r0 + sub)
        l0, l1, l2 = l0_ref[rows, :], l1_ref[rows, :], l2_ref[rows, :]
        mx = jnp.maximum(jnp.maximum(l0, l1), l2)
        e0, e1, e2 = jnp.exp(l0 - mx), jnp.exp(l1 - mx), jnp.exp(l2 - mx)
        den = e0 + e1 + e2
        od = (e0 / den) * od0_ref[rows, :] + (e1 / den) * od1_ref[rows, :] + (e2 / den) * od2_ref[rows, :]
        merged = (sig(ga_ref[rows, :]) * jnp.dot(oa_ref[rows, :], wpa_ref[...], preferred_element_type=F32)
                  + sig(gn_ref[rows, :]) * jnp.dot(on_ref[rows, :], wpn_ref[...], preferred_element_type=F32)
                  + sig(gd_ref[rows, :]) * jnp.dot(od.astype(BF16), wpd_ref[...], preferred_element_type=F32))
        y = jnp.dot(merged.astype(BF16), wout_ref[...], preferred_element_type=F32)
        xn = x_ref[rows, :] + gm_ref[...] * y
        xo_ref[rows, :] = xn
        ms = jnp.mean(xn * xn, axis=-1, keepdims=True)
        h2 = xn * lax.rsqrt(ms + NORM_EPS) * fg_ref[...]
        h2 = h2 * (1.0 + fsc_ref[...]) + fsh_ref[...]
        lg_ref[:, rows] = lax.dot_general(wr_ref[...], h2, (((1,), (1,)), ((), ())), preferred_element_type=F32,
                                          precision=HIGHEST) + br_ref[...]
        for c in range(h2.shape[1] // LANES):
            h2_ref[pl.ds(r0 * SUBLANES + c, sub, stride=SUBLANES), :] = h2[:, c * LANES:(c + 1) * LANES]


def _mix_out(x2, z, o_a, o_n, o_d, lse_d, wpa, wpn, wpd, wout, gate_m, fg, fsc, fsh, wr, br, seq, tm=512, sub=256):
    n, d = x2.shape
    assert d // LANES == SUBLANES and C_GATE % d == 0
    spb = seq // tm
    gcol = C_GATE // d
    wd = DIL_HEADS * HEAD_DIM
    row = lambda w: pl.BlockSpec((tm, w), lambda i: (i, 0))
    full = lambda a: pl.BlockSpec(a.shape, lambda i: (0,) * a.ndim)
    perb = pl.BlockSpec((None, 1, d), lambda i: (i // spb, 0, 0))
    return pl.pallas_call(
        functools.partial(_mix_out_kernel, sub=min(sub, tm)),
        grid=(n // tm,),
        in_specs=[row(d), row(o_a.shape[1]), row(o_n.shape[1]), row(wd), row(wd), row(wd), row(wd), row(wd), row(wd),
                  pl.BlockSpec((tm, d), lambda i: (i, gcol)),
                  pl.BlockSpec((tm, d), lambda i: (i, gcol + 1)),
                  pl.BlockSpec((tm, d), lambda i: (i, gcol + 2)),
                  full(wpa), full(wpn), full(wpd), full(wout),
                  perb, full(fg), perb, perb, full(wr), full(br)],
        out_specs=[row(d), pl.BlockSpec((tm * SUBLANES, LANES), lambda i: (i, 0)),
                   pl.BlockSpec((wr.shape[0], tm), lambda i: (0, i))],
        out_shape=[jax.ShapeDtypeStruct((n, d), F32),
                   jax.ShapeDtypeStruct((n * SUBLANES, LANES), F32),
                   jax.ShapeDtypeStruct((wr.shape[0], n), F32)],
        compiler_params=_params("arbitrary"),
        name="mix_out",
    )(x2, o_a, o_n, o_d[0], o_d[1], o_d[2], lse_d[0], lse_d[1], lse_d[2], z, z, z,
      wpa, wpn, wpd, wout, gate_m, fg, fsc, fsh, wr, br)


def _route_kernel(lg_ref, e_ref, w_ref, cnt_ref):
    i = pl.program_id(0)
    le = lg_ref[0:N_EXPERTS, :]
    gl = lg_ref[N_EXPERTS:N_EXPERTS + N_GROUPS, :]
    tb = le.shape[1]
    gmax = jnp.max(gl, axis=0, keepdims=True)
    grow = lax.broadcasted_iota(I32, gl.shape, 0)
    gidx = jnp.min(jnp.where(gl == gmax, grow, N_GROUPS), axis=0, keepdims=True)
    pg_top = 1.0 / jnp.sum(jnp.exp(gl - gmax), axis=0, keepdims=True)
    erow = lax.broadcasted_iota(I32, le.shape, 0)
    lm = jnp.where((erow // EXPERTS_PER_GROUP) == gidx, le, -jnp.inf)
    m1 = jnp.max(lm, axis=0, keepdims=True)
    i1 = jnp.min(jnp.where(lm == m1, erow, N_EXPERTS), axis=0, keepdims=True)
    lm2 = jnp.where(erow == i1, -jnp.inf, lm)
    m2 = jnp.max(lm2, axis=0, keepdims=True)
    i2 = jnp.min(jnp.where(lm2 == m2, erow, N_EXPERTS), axis=0, keepdims=True)
    t = jnp.exp(m2 - m1)
    e_ref[0:1, :] = i1
    e_ref[1:2, :] = i2
    w_ref[0:1, :] = pg_top / (1.0 + t)
    w_ref[1:2, :] = pg_top * t / (1.0 + t)
    oh = (erow == i1).astype(F32) + (erow == i2).astype(F32)
    cnt = jnp.sum(oh, axis=1, keepdims=True)

    @pl.when(i == 0)
    def _():
        cnt_ref[...] = jnp.zeros_like(cnt_ref)

    cnt_ref[...] += jnp.broadcast_to(cnt, cnt_ref.shape)


def _route(lgt, tb=1024):
    rows, n = lgt.shape
    tb = min(tb, n)
    return pl.pallas_call(
        _route_kernel,
        grid=(n // tb,),
        in_specs=[pl.BlockSpec((rows, tb), lambda i: (0, i))],
        out_specs=[pl.BlockSpec((2, tb), lambda i: (0, i)),
                   pl.BlockSpec((2, tb), lambda i: (0, i)),
                   pl.BlockSpec((N_EXPERTS, LANES), lambda i: (0, 0))],
        out_shape=[jax.ShapeDtypeStruct((2, n), I32),
                   jax.ShapeDtypeStruct((2, n), F32),
                   jax.ShapeDtypeStruct((N_EXPERTS, LANES), F32)],
        compiler_params=_params("arbitrary"),
        name="route",
    )(lgt)


def _rank_kernel(e_ref, cnt_ref, dest_ref, be_ref, nu_ref, zs_ref, carry_scr, base_scr, *, blk):
    i = pl.program_id(0)

    @pl.when(i == 0)
    def _():
        cnt = cnt_ref[...]
        padded = jnp.floor((cnt + (blk - 1)) / blk) * blk
        r = lax.broadcasted_iota(I32, cnt.shape, 0)
        c = lax.broadcasted_iota(I32, cnt.shape, 1)
        prow = jnp.sum(jnp.where(r == c, padded, 0.0), axis=0, keepdims=True)
        pad_end = jnp.sum(jnp.where(c <= r, prow, 0.0), axis=1, keepdims=True)
        base = jnp.broadcast_to(pad_end, cnt.shape) - padded
        base_scr[...] = base
        carry_scr[...] = jnp.zeros_like(carry_scr)
        zs_ref[...] = jnp.sum(jnp.where(r == c, base + cnt, 0.0) + jnp.where(r + N_EXPERTS == c, padded - cnt, 0.0),
                              axis=0, keepdims=True).astype(I32)
        jb = lax.broadcasted_iota(I32, (N_EXPERTS, be_ref.shape[1]), 1).astype(F32) * blk
        be = jnp.sum((pad_end <= jb).astype(I32), axis=0, keepdims=True)
        be_ref[...] = jnp.minimum(be, N_EXPERTS - 1)
        nu_ref[...] = jnp.broadcast_to(jnp.max(pad_end, axis=0, keepdims=True) / blk, nu_ref.shape).astype(I32)

    e = e_ref[...]
    tb = e.shape[1]
    erow = lax.broadcasted_iota(I32, (N_EXPERTS, tb), 0)
    oh0 = erow == e[0:1]
    oh1 = erow == e[1:2]
    both = jnp.where(oh0 | oh1, 1.0, 0.0)
    upper = jnp.where(lax.broadcasted_iota(I32, (tb, tb), 0) < lax.broadcasted_iota(I32, (tb, tb), 1), 1.0, 0.0)
    prefix = jnp.dot(both.astype(BF16), upper.astype(BF16), preferred_element_type=F32)
    tot = prefix + base_scr[:, 0:1] + carry_scr[:, 0:1]
    dest_ref[0:1, :] = jnp.sum(jnp.where(oh0, tot, 0.0), axis=0, keepdims=True).astype(I32)
    dest_ref[1:2, :] = jnp.sum(jnp.where(oh1, tot, 0.0), axis=0, keepdims=True).astype(I32)
    carry_scr[...] += jnp.broadcast_to(jnp.sum(both, axis=1, keepdims=True), carry_scr.shape)


def _rank(eidx, cnt, n_blocks, blk, tb=512):
    n = eidx.shape[1]
    tb = min(tb, n)
    nbp = -(-n_blocks // LANES) * LANES
    return pl.pallas_call(
        functools.partial(_rank_kernel, blk=blk),
        grid=(n // tb,),
        in_specs=[pl.BlockSpec((2, tb), lambda i: (0, i)),
                  pl.BlockSpec(cnt.shape, lambda i: (0, 0))],
        out_specs=[pl.BlockSpec((2, tb), lambda i: (0, i)),
                   pl.BlockSpec((1, nbp), lambda i: (0, 0)),
                   pl.BlockSpec((1, LANES), lambda i: (0, 0)),
                   pl.BlockSpec((1, LANES), lambda i: (0, 0))],
        out_shape=[jax.ShapeDtypeStruct((2, n), I32),
                   jax.ShapeDtypeStruct((1, nbp), I32),
                   jax.ShapeDtypeStruct((1, LANES), I32),
                   jax.ShapeDtypeStruct((1, LANES), I32)],
        scratch_shapes=[pltpu.VMEM((N_EXPERTS, LANES), F32), pltpu.VMEM((N_EXPERTS, LANES), F32)],
        compiler_params=_params("arbitrary"),
        name="rank",
    )(eidx, cnt)


def _tile_copy(src, src_row, dst, dst_row, sem):
    return pltpu.make_async_copy(src.at[pl.ds(pl.multiple_of(src_row * SUBLANES, SUBLANES), SUBLANES), :],
                                 dst.at[pl.ds(pl.multiple_of(dst_row * SUBLANES, SUBLANES), SUBLANES), :], sem)


def _dispatch_kernel(zs_ref, nu_ref, dest_ref, h_ref, xs_hbm, zero_scr, sem, zsem, *, tb, blk, total_blocks):
    def zero_fill(slot, nslots):
        start = pl.multiple_of(slot * SUBLANES, SUBLANES)
        return pltpu.make_async_copy(zero_scr.at[pl.ds(0, nslots * SUBLANES), :],
                                     xs_hbm.at[pl.ds(start, nslots * SUBLANES), :], zsem)

    def pad_fills(act):
        for e in range(N_EXPERTS):
            off = zs_ref[e]
            count = zs_ref[N_EXPERTS + e]
            for b in range(blk.bit_length() - 1):
                bit = (count >> b) & 1

                @pl.when(bit == 1)
                def _(off=off, b=b):
                    act(zero_fill(off, 1 << b))

                off = off + bit * (1 << b)

    @pl.when(pl.program_id(0) == 0)
    def _():
        zero_scr[...] = jnp.zeros_like(zero_scr)
        pad_fills(lambda cp: cp.start())

        def tail_start(j, carry):
            zero_fill(j * blk, blk).start()
            return carry

        def tail_wait(j, carry):
            zero_fill(j * blk, blk).wait()
            return carry

        lax.fori_loop(nu_ref[0], total_blocks, tail_start, 0)
        pad_fills(lambda cp: cp.wait())
        lax.fori_loop(nu_ref[0], total_blocks, tail_wait, 0)

    def issue(t, carry):
        for k in range(2):
            _tile_copy(h_ref, t, xs_hbm, dest_ref[k, t], sem).start()
        return carry

    lax.fori_loop(0, tb, issue, 0)
    for k in range(2):
        pltpu.make_async_copy(h_ref, xs_hbm.at[pl.ds(0, tb * SUBLANES), :], sem).wait()


def _dispatch(zstart, n_used, dest, h2t, n_blocks, blk, tb=512):
    n = dest.shape[1]
    tb = min(tb, n)
    grid_spec = pltpu.PrefetchScalarGridSpec(
        num_scalar_prefetch=2,
        grid=(n // tb,),
        in_specs=[pl.BlockSpec((2, tb), lambda i, zs, nu: (0, i), memory_space=pltpu.SMEM),
                  pl.BlockSpec((tb * SUBLANES, LANES), lambda i, zs, nu: (i, 0))],
        out_specs=pl.BlockSpec(memory_space=pl.ANY),
        scratch_shapes=[pltpu.VMEM((blk * SUBLANES, LANES), F32), pltpu.SemaphoreType.DMA, pltpu.SemaphoreType.DMA],
    )
    total_blocks = n_blocks
    return pl.pallas_call(
        functools.partial(_dispatch_kernel, tb=tb, blk=blk, total_blocks=total_blocks),
        grid_spec=grid_spec,
        out_shape=jax.ShapeDtypeStruct((total_blocks * blk * SUBLANES, LANES), F32),
        compiler_params=pltpu.CompilerParams(dimension_semantics=("arbitrary",), has_side_effects=True),
        name="dispatch",
    )(zstart, n_used, dest, h2t)


def _expert_kernel(be_ref, first_ref, nu_ref, xs_ref, w1_ref, w3_ref, w2_ref, y_ref, xb_scr, w1_scr, w3_scr, w2_scr, *, blk):
    del be_ref
    j = pl.program_id(0)
    nchunk = xb_scr.shape[1] // LANES

    @pl.when(j < nu_ref[0])
    def _():
        @pl.when(first_ref[j] == 1)
        def _():
            w1_scr[...] = w1_ref[...].astype(BF16)
            w3_scr[...] = w3_ref[...].astype(BF16)
            w2_scr[...] = w2_ref[...].astype(BF16)

        for c in range(nchunk):
            xb_scr[:, c * LANES:(c + 1) * LANES] = xs_ref[pl.ds(c, blk, stride=SUBLANES), :].astype(BF16)
        xb = xb_scr[...]
        a = jnp.dot(xb, w1_scr[...], preferred_element_type=F32)
        b = jnp.dot(xb, w3_scr[...], preferred_element_type=F32)
        hid = (a / (1.0 + jnp.exp(-a))) * b
        y = jnp.dot(hid.astype(BF16), w2_scr[...], preferred_element_type=F32)
        for c in range(nchunk):
            y_ref[pl.ds(c, blk, stride=SUBLANES), :] = y[:, c * LANES:(c + 1) * LANES]

    @pl.when(j >= nu_ref[0])
    def _():
        y_ref[...] = jnp.zeros_like(y_ref)


def _experts(block_e, first, n_used, xs, w1, w3, w2, layer, n_blocks, blk):
    d, hid = w1.shape[2], w1.shape[3]
    used = lambda j, nu: jnp.minimum(j, nu[0] - 1)
    grid_spec = pltpu.PrefetchScalarGridSpec(
        num_scalar_prefetch=3,
        grid=(n_blocks,),
        in_specs=[pl.BlockSpec((blk * SUBLANES, LANES), lambda j, be, fi, nu: (used(j, nu), 0)),
                  pl.BlockSpec((None, None, d, hid), lambda j, be, fi, nu: (layer, be[j], 0, 0)),
                  pl.BlockSpec((None, None, d, hid), lambda j, be, fi, nu: (layer, be[j], 0, 0)),
                  pl.BlockSpec((None, None, hid, d), lambda j, be, fi, nu: (layer, be[j], 0, 0))],
        out_specs=pl.BlockSpec((blk * SUBLANES, LANES), lambda j, be, fi, nu: (j, 0)),
        scratch_shapes=[pltpu.VMEM((blk, d), BF16), pltpu.VMEM((d, hid), BF16), pltpu.VMEM((d, hid), BF16),
                        pltpu.VMEM((hid, d), BF16)],
    )
    return pl.pallas_call(
        functools.partial(_expert_kernel, blk=blk),
        grid_spec=grid_spec,
        out_shape=jax.ShapeDtypeStruct((n_blocks * blk * SUBLANES, LANES), F32),
        compiler_params=_params("arbitrary"),
        name="experts",
    )(block_e, first, n_used, xs, w1, w3, w2)


def _combine_kernel(dcur_ref, dnext_ref, x_ref, w_ref, gf_ref, fg_ref, y_hbm, xo_ref, g_scr, sems, *, tb, nsteps):
    i = pl.program_id(0)
    slot = i % 2

    def gather(d_ref, s):
        def issue(t, carry):
            for k in range(2):
                _tile_copy(y_hbm, d_ref[k, t], g_scr.at[s, k], t, sems.at[s]).start()
            return carry

        lax.fori_loop(0, tb, issue, 0)

    @pl.when(i == 0)
    def _():
        gather(dcur_ref, 0)

    @pl.when(i + 1 < nsteps)
    def _():
        gather(dnext_ref, 1 - slot)

    for k in range(2):
        pltpu.make_async_copy(y_hbm.at[pl.ds(0, tb * SUBLANES), :], g_scr.at[slot, k], sems.at[slot]).wait()
    w0 = w_ref[:, 0:1]
    w1 = w_ref[:, 1:2]
    for c in range(x_ref.shape[1] // LANES):
        sl = slice(c * LANES, (c + 1) * LANES)
        yc = (w0 * g_scr[slot, 0, pl.ds(c, tb, stride=SUBLANES), :]
              + w1 * g_scr[slot, 1, pl.ds(c, tb, stride=SUBLANES), :])
        xo_ref[:, sl] = x_ref[:, sl] + gf_ref[:, sl] * yc
    xn = xo_ref[...]
    ms = jnp.mean(xn * xn, axis=-1, keepdims=True)
    xo_ref[...] = xn * lax.rsqrt(ms + NORM_EPS) * fg_ref[...]


def _combine_final(dest, x2, wts_t, gate_f, y, final_g, seq, tb=256):
    n, d = x2.shape
    spb = seq // tb
    nsteps = n // tb
    return pl.pallas_call(
        functools.partial(_combine_kernel, tb=tb, nsteps=nsteps),
        grid=(nsteps,),
        in_specs=[pl.BlockSpec((2, tb), lambda i: (0, i), memory_space=pltpu.SMEM),
                  pl.BlockSpec((2, tb), lambda i: (0, jnp.minimum(i + 1, nsteps - 1)), memory_space=pltpu.SMEM),
                  pl.BlockSpec((tb, d), lambda i: (i, 0)),
                  pl.BlockSpec((tb, 2), lambda i: (i, 0)),
                  pl.BlockSpec((None, 1, d), lambda i: (i // spb, 0, 0)),
                  pl.BlockSpec((1, d), lambda i: (0, 0)),
                  pl.BlockSpec(memory_space=pl.ANY)],
        out_specs=pl.BlockSpec((tb, d), lambda i: (i, 0)),
        out_shape=jax.ShapeDtypeStruct((n, d), F32),
        scratch_shapes=[pltpu.VMEM((2, 2, tb * SUBLANES, LANES), F32), pltpu.SemaphoreType.DMA((2,))],
        compiler_params=_params("arbitrary"),
        name="combine_final",
    )(dest, dest, x2, wts_t, gate_f, final_g, y)


def _rotary_tables(seq, width):
    inv = 1.0 / (ROPE_THETA ** (jnp.arange(0, HEAD_DIM, 2, dtype=F32) / HEAD_DIM))
    ang = jnp.arange(seq, dtype=F32)[:, None] * inv[None, :]
    ang = jnp.concatenate([ang, ang], axis=-1)
    sign = jnp.concatenate([-jnp.ones((HEAD_DIM // 2,), F32), jnp.ones((HEAD_DIM // 2,), F32)])
    reps = width // HEAD_DIM
    return jnp.tile(jnp.cos(ang), (1, reps)), jnp.tile(jnp.sin(ang) * sign[None, :], (1, reps))


def _col_scale():
    colscale = jnp.ones((1, NC), F32)
    for lo, hi in ((C_QA, C_KA), (C_QD, C_KD), (C_QN, C_KN)):
        colscale = colscale.at[:, lo:hi].set(HEAD_DIM ** -0.5)
    return colscale.at[:, C_QA:C_KA].multiply(math.log2(math.e))


def kernel(x, c, ada_w, ada_b, mix_norm_g, ffn_norm_g, w_in, da_lambda, da_subln_g, na_rpb, w_proj_a, w_proj_n, w_proj_d, w_out, router_group_w, router_group_b, router_expert_w, router_expert_b, expert_w1, expert_w3, expert_w2, final_norm_g):
    bsz, seq, d = x.shape
    depth = ada_w.shape[0]
    n = bsz * seq
    tn = 512
    assert w_in.shape[2] == NC

    mods = _ada(c, ada_w, ada_b).reshape(depth, bsz, 6, 1, d)
    cos_t, sin_t = _rotary_tables(seq, LANES)
    colscale = _col_scale()

    n_blocks = (2 * n) // MOE_BLK + N_EXPERTS
    rpad = SUBLANES - N_GROUPS

    x2 = x.reshape(n, d)
    moe = None
    for l in range(depth):
        lam_init = 0.8 - 0.6 * math.exp(-0.3 * l)
        shift_m, scale_m, gate_m, shift_f, scale_f, gate_f = (mods[l, :, k] for k in range(6))
        w_bf = jnp.concatenate([w_in[l, :, lo:hi] for lo, hi in IN_PERM], axis=1).astype(BF16)
        proj_args = (mix_norm_g[l].reshape(1, d), scale_m, shift_m, cos_t, sin_t, colscale, w_bf, seq)
        if moe is None:
            z = _inproj(x2, *proj_args, tn=tn)
        else:
            x2, z = _combine_inproj(moe[0], x2, *moe[1:], *proj_args, tn=tn)
        o_a = _diff_attn(z, da_lambda[l], da_subln_g[l].reshape(1, 2 * HEAD_DIM), lam_init, bsz, seq)
        o_n = _na_attn(z, _na_bias_table(na_rpb[l]), bsz, seq)
        dil = [_dil_attn(z, g, bsz, seq) for g in range(len(DIL_DILATIONS))]
        wr = jnp.concatenate([router_expert_w[l], router_group_w[l], jnp.zeros((d, rpad), F32)], axis=1).T
        br = jnp.concatenate([router_expert_b[l], router_group_b[l], jnp.zeros((rpad,), F32)]).reshape(-1, 1)
        x2, h2t, lgt = _mix_out(
            x2, z, o_a, o_n, [o for o, _ in dil], [s for _, s in dil],
            w_proj_a[l].astype(BF16), w_proj_n[l].astype(BF16), w_proj_d[l].astype(BF16), w_out[l].astype(BF16),
            gate_m, ffn_norm_g[l].reshape(1, d), scale_f, shift_f, wr, br, seq)
        eidx, wts, cnt = _route(lgt)
        dest, block_e, n_used, zstart = _rank(eidx, cnt, n_blocks, MOE_BLK)
        block_e = block_e[0, :n_blocks]
        first = jnp.concatenate([jnp.ones((1,), I32), (block_e[1:] != block_e[:-1]).astype(I32)])
        n_used = n_used[0, :1]
        xs = _dispatch(zstart[0, :2 * N_EXPERTS], n_used, dest, h2t, n_blocks, MOE_BLK)
        y = _experts(block_e, first, n_used, xs, expert_w1, expert_w3, expert_w2, l, n_blocks, MOE_BLK)
        moe = (dest, wts.T, gate_f, y)
    return _combine_final(moe[0], x2, *moe[1:], final_norm_g.reshape(1, d), seq).reshape(bsz, seq, d)
```

```python
import functools
import math

import jax
import jax.numpy as jnp
from jax import lax
from jax.experimental import pallas as pl
from jax.experimental.pallas import tpu as pltpu

F32 = jnp.float32
BF16 = jnp.bfloat16
I32 = jnp.int32
HIGHEST = lax.Precision.HIGHEST

HEAD_DIM = 64
ROPE_THETA = 10000.0
NORM_EPS = 1e-6
NEG_INF = -1e30

DA_HEADS = 4
GRID_W = 64
NA_HEADS = 4
NA_WIN_ROWS = 8
NA_WIN_COLS = 16
DIL_DILATIONS = (1, 4, 16)
DIL_SIDE = 64
DIL_HEADS = 4
DIL_MAX_STEP_ROWS = 2048
N_GROUPS = 4
EXPERTS_PER_GROUP = 8
N_EXPERTS = N_GROUPS * EXPERTS_PER_GROUP

LANES = 128
SUBLANES = 8
MOE_BLK = 512
VMEM_LIMIT = 56 * 1024 * 1024

C_QA, C_KA, C_QD, C_KD = 0, 512, 1024, 1792
N_ROT = 2560
C_VA, C_GATE, C_QN, C_KN, C_VN, C_VD = 2560, 3072, 6144, 6400, 6656, 6912
NC = 7680
IN_PERM = ((0, 1024), (2304, 3840), (1024, 1536), (4608, 7680), (1536, 2304), (3840, 4608))


def _nt_dot(a, b):
    return lax.dot_general(a, b, (((1,), (1,)), ((), ())), preferred_element_type=F32)


def _stack_heads(q, nh):
    lane = lax.broadcasted_iota(I32, q.shape, 1)
    masks = [(lane >= HEAD_DIM * h) & (lane < HEAD_DIM * (h + 1)) for h in range(nh)]
    return jnp.concatenate([jnp.where(hm, q, jnp.zeros_like(q)) for hm in masks], axis=0), masks


def _unstack_heads(x, masks):
    m = x.shape[0] // len(masks)
    out = x[0:m]
    for h in range(1, len(masks)):
        out = jnp.where(masks[h], x[h * m:(h + 1) * m], out)
    return out


def _params(*sem):
    return pltpu.CompilerParams(dimension_semantics=sem, vmem_limit_bytes=VMEM_LIMIT)


def _ada_kernel(c_ref, w_ref, b_ref, o_ref):
    c = c_ref[...]
    ca = c / (1.0 + jnp.exp(-c))
    o_ref[...] = jnp.dot(ca, w_ref[...], preferred_element_type=F32, precision=HIGHEST) + b_ref[...]


def _ada(c, ada_w, ada_b):
    depth, d, w6 = ada_w.shape
    bsz = c.shape[0]
    rows = -(-bsz // SUBLANES) * SUBLANES
    cp = jnp.zeros((rows, d), F32).at[:bsz].set(c)
    tn = w6 // 4
    out = pl.pallas_call(
        _ada_kernel,
        grid=(depth, w6 // tn),
        in_specs=[pl.BlockSpec((rows, d), lambda l, j: (0, 0)),
                  pl.BlockSpec((None, d, tn), lambda l, j: (l, 0, j)),
                  pl.BlockSpec((None, 1, tn), lambda l, j: (l, 0, j))],
        out_specs=pl.BlockSpec((None, rows, tn), lambda l, j: (l, 0, j)),
        out_shape=jax.ShapeDtypeStruct((depth, rows, w6), F32),
        compiler_params=_params("arbitrary", "arbitrary"),
        name="ada_mod",
    )(cp, ada_w, ada_b.reshape(depth, 1, w6))
    return out[:, :bsz]


def _norm_modulate(x, g_ref, sc_ref, sh_ref):
    ms = jnp.mean(x * x, axis=-1, keepdims=True)
    y = x * lax.rsqrt(ms + NORM_EPS) * g_ref[...]
    return y * (1.0 + sc_ref[...]) + sh_ref[...]


def _project_chunk(j, tn, h_scr, w_scr, cs_ref, cos_ref, sin_ref, z_ref):
    cols = slice(j * tn, (j + 1) * tn)
    acc = jnp.dot(h_scr[...], w_scr[:, cols], preferred_element_type=F32) * cs_ref[:, cols]
    if j < N_ROT // tn:
        half = HEAD_DIM // 2
        lane = lax.broadcasted_iota(I32, (acc.shape[0], LANES), 1)
        lo = (lane % HEAD_DIM) < half
        cos = cos_ref[...]
        sin = sin_ref[...]
        for cc in range(tn // LANES):
            a = acc[:, cc * LANES:(cc + 1) * LANES]
            rot = jnp.where(lo, pltpu.roll(a, LANES - half, 1), pltpu.roll(a, half, 1))
            c0 = j * tn + cc * LANES
            z_ref[:, c0:c0 + LANES] = (a * cos + rot * sin).astype(BF16)
    else:
        z_ref[:, cols] = acc.astype(BF16)


def _inproj_kernel(x_ref, g_ref, sc_ref, sh_ref, cos_ref, sin_ref, cs_ref, w_hbm, z_ref, h_scr, w_scr, sem, *, tn):
    @pl.when(pl.program_id(0) == 0)
    def _():
        cp = pltpu.make_async_copy(w_hbm, w_scr, sem)
        cp.start()
        cp.wait()

    h_scr[...] = _norm_modulate(x_ref[...], g_ref, sc_ref, sh_ref).astype(BF16)
    for j in range(w_scr.shape[1] // tn):
        _project_chunk(j, tn, h_scr, w_scr, cs_ref, cos_ref, sin_ref, z_ref)


def _combine_inproj_kernel(dcur_ref, dnext_ref, x_ref, wt_ref, gf_ref, y_hbm,
                           g_ref, sc_ref, sh_ref, cos_ref, sin_ref, cs_ref, w_hbm,
                           xo_ref, z_ref, h_scr, w_scr, g_scr, wsem, gsem, *, tn, tb, nsteps):
    i = pl.program_id(0)

    @pl.when(i == 0)
    def _():
        cp = pltpu.make_async_copy(w_hbm, w_scr, wsem)
        cp.start()

        def issue(t, carry):
            for k in range(2):
                _tile_copy(y_hbm, dcur_ref[k, t], g_scr.at[k], t, gsem).start()
            return carry

        lax.fori_loop(0, tb, issue, 0)
        cp.wait()

    for k in range(2):
        pltpu.make_async_copy(y_hbm.at[pl.ds(0, tb * SUBLANES), :], g_scr.at[k], gsem).wait()
    w0 = wt_ref[:, 0:1]
    w1 = wt_ref[:, 1:2]
    for c in range(x_ref.shape[1] // LANES):
        sl = slice(c * LANES, (c + 1) * LANES)
        yc = w0 * g_scr[0, pl.ds(c, tb, stride=SUBLANES), :] + w1 * g_scr[1, pl.ds(c, tb, stride=SUBLANES), :]
        xo_ref[:, sl] = x_ref[:, sl] + gf_ref[:, sl] * yc
    h_scr[...] = _norm_modulate(xo_ref[...], g_ref, sc_ref, sh_ref).astype(BF16)

    nchunks = w_scr.shape[1] // tn
    per = -(-tb // nchunks)
    for j in range(nchunks):
        _project_chunk(j, tn, h_scr, w_scr, cs_ref, cos_ref, sin_ref, z_ref)
        for t in range(j * per, min((j + 1) * per, tb)):
            for k in range(2):
                _tile_copy(y_hbm, dnext_ref[k, t], g_scr.at[k], t, gsem).start()

    @pl.when(i == nsteps - 1)
    def _():
        for k in range(2):
            pltpu.make_async_copy(y_hbm.at[pl.ds(0, tb * SUBLANES), :], g_scr.at[k], gsem).wait()


def _combine_inproj(dest, x2, wts_t, gate_f, y, g, scale, shift, cos_t, sin_t, colscale, w_bf, seq, tm=512, tn=512):
    n, d = x2.shape
    nc = w_bf.shape[1]
    assert N_ROT % tn == 0 and nc % tn == 0
    spb = seq // tm
    nsteps = n // tm
    perb = pl.BlockSpec((None, 1, d), lambda i: (i // spb, 0, 0))
    return pl.pallas_call(
        functools.partial(_combine_inproj_kernel, tn=tn, tb=tm, nsteps=nsteps),
        grid=(nsteps,),
        in_specs=[pl.BlockSpec((2, tm), lambda i: (0, i), memory_space=pltpu.SMEM),
                  pl.BlockSpec((2, tm), lambda i: (0, jnp.minimum(i + 1, nsteps - 1)), memory_space=pltpu.SMEM),
                  pl.BlockSpec((tm, d), lambda i: (i, 0)),
                  pl.BlockSpec((tm, 2), lambda i: (i, 0)),
                  perb,
                  pl.BlockSpec(memory_space=pl.ANY),
                  pl.BlockSpec((1, d), lambda i: (0, 0)),
                  perb, perb,
                  pl.BlockSpec((tm, LANES), lambda i: (i % spb, 0)),
                  pl.BlockSpec((tm, LANES), lambda i: (i % spb, 0)),
                  pl.BlockSpec((1, nc), lambda i: (0, 0)),
                  pl.BlockSpec(memory_space=pl.ANY)],
        out_specs=[pl.BlockSpec((tm, d), lambda i: (i, 0)), pl.BlockSpec((tm, nc), lambda i: (i, 0))],
        out_shape=[jax.ShapeDtypeStruct((n, d), F32), jax.ShapeDtypeStruct((n, nc), BF16)],
        scratch_shapes=[pltpu.VMEM((tm, d), BF16), pltpu.VMEM((d, nc), BF16),
                        pltpu.VMEM((2, tm * SUBLANES, LANES), F32),
                        pltpu.SemaphoreType.DMA, pltpu.SemaphoreType.DMA],
        compiler_params=_params("arbitrary"),
        name="combine_inproj",
    )(dest, dest, x2, wts_t, gate_f, y, g, scale, shift, cos_t, sin_t, colscale, w_bf)


def _inproj(x2, g, scale, shift, cos_t, sin_t, colscale, w_bf, seq, tm=512, tn=512):
    n, d = x2.shape
    nc = w_bf.shape[1]
    assert N_ROT % tn == 0 and nc % tn == 0
    spb = seq // tm
    return pl.pallas_call(
        functools.partial(_inproj_kernel, tn=tn),
        grid=(n // tm,),
        in_specs=[pl.BlockSpec((tm, d), lambda i: (i, 0)),
                  pl.BlockSpec((1, d), lambda i: (0, 0)),
                  pl.BlockSpec((None, 1, d), lambda i: (i // spb, 0, 0)),
                  pl.BlockSpec((None, 1, d), lambda i: (i // spb, 0, 0)),
                  pl.BlockSpec((tm, LANES), lambda i: (i % spb, 0)),
                  pl.BlockSpec((tm, LANES), lambda i: (i % spb, 0)),
                  pl.BlockSpec((1, nc), lambda i: (0, 0)),
                  pl.BlockSpec(memory_space=pl.ANY)],
        out_specs=pl.BlockSpec((tm, nc), lambda i: (i, 0)),
        out_shape=jax.ShapeDtypeStruct((n, nc), BF16),
        scratch_shapes=[pltpu.VMEM((tm, d), BF16), pltpu.VMEM((d, nc), BF16), pltpu.SemaphoreType.DMA],
        compiler_params=_params("arbitrary"),
        name="inproj",
    )(x2, g, scale, shift, cos_t, sin_t, colscale, w_bf)


def _diff_attn_kernel(lam_ref, g_ref, q_ref, k_ref, v_ref, o_ref, *, lam_init, chunk):
    lp = lam_ref[...]
    lam = (jnp.exp(jnp.sum(lp[0:1] * lp[1:2], axis=-1, keepdims=True))
           - jnp.exp(jnp.sum(lp[2:3] * lp[3:4], axis=-1, keepdims=True)) + lam_init)
    k = k_ref[...]
    v = v_ref[...]
    lane = lax.broadcasted_iota(I32, (chunk, 2 * HEAD_DIM), 1)
    chains = [(c, m) for c in range(q_ref.shape[0] // chunk) for m in range(2)]

    def scores(c, m):
        q = q_ref[c * chunk:(c + 1) * chunk, :]
        qm = jnp.where((lane >= HEAD_DIM * m) & (lane < HEAD_DIM * (m + 1)), q, jnp.zeros_like(q))
        return _nt_dot(qm, k)

    s_next = scores(*chains[0])
    pb, ls = {}, {}
    for i, (c, m) in enumerate(chains):
        s = s_next
        if i + 1 < len(chains):
            s_next = scores(*chains[i + 1])
        mx = jnp.max(s, axis=-1, keepdims=True)
        p = jnp.exp2(s - mx)
        ls[m] = jnp.sum(p, axis=-1, keepdims=True)
        pb[m] = p.astype(BF16)
        if m == 1:
            ratio = (lam * ls[0] / ls[1]).astype(BF16)
            a = pb[0] - pb[1] * ratio
            o = jnp.dot(a, v, preferred_element_type=F32) / ls[0]
            ms = jnp.mean(o * o, axis=-1, keepdims=True)
            o = o * lax.rsqrt(ms + NORM_EPS) * g_ref[...] * (1.0 - lam_init)
            o_ref[c * chunk:(c + 1) * chunk, :] = o.astype(BF16)


def _diff_attn(z, lam_p, subln_g, lam_init, bsz, seq, tq=1024, chunk=256):
    n = z.shape[0]
    hw = 2 * HEAD_DIM
    nq = seq // tq
    return pl.pallas_call(
        functools.partial(_diff_attn_kernel, lam_init=lam_init, chunk=min(chunk, tq)),
        grid=(bsz, DA_HEADS, nq),
        in_specs=[pl.BlockSpec((4, HEAD_DIM), lambda b, h, i: (0, 0)),
                  pl.BlockSpec((1, hw), lambda b, h, i: (0, 0)),
                  pl.BlockSpec((tq, hw), lambda b, h, i: (b * nq + i, C_QA // hw + h)),
                  pl.BlockSpec((seq, hw), lambda b, h, i: (b, C_KA // hw + h)),
                  pl.BlockSpec((seq, hw), lambda b, h, i: (b, C_VA // hw + h))],
        out_specs=pl.BlockSpec((tq, hw), lambda b, h, i: (b * nq + i, h)),
        out_shape=jax.ShapeDtypeStruct((n, DA_HEADS * hw), BF16),
        compiler_params=_params("arbitrary", "arbitrary", "arbitrary"),
        name="diff_attn",
    )(lam_p, subln_g, z, z, z)


def _na_kernel(q_ref, k_ref, v_ref, bias_ref, o_ref, *, rows):
    win = NA_WIN_ROWS * GRID_W

    def body(r, carry):
        kr0 = jnp.clip(r - NA_WIN_ROWS // 2, 0, rows - NA_WIN_ROWS)
        pat = kr0 - r + (NA_WIN_ROWS - 1)
        q4, masks = _stack_heads(q_ref[pl.ds(pl.multiple_of(r * GRID_W, GRID_W), GRID_W), :], NA_HEADS)
        kw = k_ref[pl.ds(pl.multiple_of(kr0 * GRID_W, GRID_W), win), :]
        vw = v_ref[pl.ds(pl.multiple_of(kr0 * GRID_W, GRID_W), win), :]
        s = _nt_dot(q4, kw) + bias_ref[pat]
        mx = jnp.max(s, axis=-1, keepdims=True)
        p = jnp.exp(s - mx)
        l = jnp.sum(p, axis=-1, keepdims=True)
        o4 = jnp.dot(p.astype(BF16), vw, preferred_element_type=F32) / l
        o_ref[pl.ds(pl.multiple_of(r * GRID_W, GRID_W), GRID_W), :] = _unstack_heads(o4, masks).astype(BF16)
        return carry

    lax.fori_loop(0, rows, body, 0, unroll=4)


def _na_bias_table(rpb):
    c = jnp.arange(GRID_W)[:, None]
    kc = jnp.arange(GRID_W)[None, :]
    kc0 = jnp.clip(c - NA_WIN_COLS // 2, 0, GRID_W - NA_WIN_COLS)
    valid = (kc >= kc0) & (kc < kc0 + NA_WIN_COLS)
    off = GRID_W - NA_WIN_COLS
    padded = jnp.pad(rpb.astype(F32), ((0, 0), (0, 0), (off, off)))
    cols = jnp.stack([padded[:, :, GRID_W - 1 - q:2 * GRID_W - 1 - q] for q in range(GRID_W)], axis=2)
    cols = jnp.where(valid[None, None], cols, NEG_INF)
    tab = jnp.stack([cols[:, p:p + NA_WIN_ROWS] for p in range(NA_WIN_ROWS)], axis=0)
    tab = tab.transpose(0, 1, 3, 2, 4)
    return tab.reshape(NA_WIN_ROWS, NA_HEADS * GRID_W, NA_WIN_ROWS * GRID_W)


def _na_attn(z, bias_tab, bsz, seq):
    n = z.shape[0]
    width = NA_HEADS * HEAD_DIM
    rows = seq // GRID_W
    assert rows >= NA_WIN_ROWS
    return pl.pallas_call(
        functools.partial(_na_kernel, rows=rows),
        grid=(bsz,),
        in_specs=[pl.BlockSpec((seq, width), lambda b: (b, C_QN // width)),
                  pl.BlockSpec((seq, width), lambda b: (b, C_KN // width)),
                  pl.BlockSpec((seq, width), lambda b: (b, C_VN // width)),
                  pl.BlockSpec(bias_tab.shape, lambda b: (0, 0, 0))],
        out_specs=pl.BlockSpec((seq, width), lambda b: (b, 0)),
        out_shape=jax.ShapeDtypeStruct((n, width), BF16),
        compiler_params=_params("arbitrary"),
        name="na_attn",
    )(z, z, z, bias_tab)


def _dil_kernel(q_ref, k_ref, v_ref, o_ref, lse_ref, *scratch, dil, seg, win, tq, nqb):
    width = DIL_HEADS * HEAD_DIM
    nlc = width // LANES
    qi = pl.program_id(1)
    if dil > 1:
        kd_scr, vd_scr, stage_scr, ostage_scr, lstage_scr = scratch

        @pl.when(qi == 0)
        def _():
            for src, dst in ((k_ref, kd_scr), (v_ref, vd_scr)):
                for lc in range(nlc):
                    stage_scr[lc] = src[:, lc * LANES:(lc + 1) * LANES].astype(F32)
                for r in range(dil):
                    for lc in range(nlc):
                        dst[r * seg:(r + 1) * seg, lc * LANES:(lc + 1) * LANES] = (
                            stage_scr[lc, pl.ds(r, seg, stride=dil), :].astype(BF16))

        for lc in range(nlc):
            stage_scr[lc, 0:nqb * tq * dil, :] = q_ref[:, lc * LANES:(lc + 1) * LANES].astype(F32)
        k_src, v_src = kd_scr, vd_scr
    else:
        k_src, v_src = k_ref, v_ref

    def block(r, u):
        a0 = (qi * nqb + u) * tq
        w0 = jnp.clip(a0 - DIL_SIDE, 0, seg - win)
        aq = a0 + lax.broadcasted_iota(I32, (DIL_HEADS * tq, win), 0) % tq
        ak = w0 + lax.broadcasted_iota(I32, (DIL_HEADS * tq, win), 1)
        valid = jnp.abs(aq - ak) <= DIL_SIDE
        row0 = u * tq * dil
        if dil > 1:
            q = jnp.concatenate([stage_scr[lc, pl.ds(row0 + r, tq, stride=dil), :] for lc in range(nlc)],
                                axis=1).astype(BF16)
        else:
            q = q_ref[row0:row0 + tq, :]
        start = pl.multiple_of(r * seg + w0, DIL_SIDE)
        kw = k_src[pl.ds(start, win), :]
        vw = v_src[pl.ds(start, win), :]
        q4, masks = _stack_heads(q, DIL_HEADS)
        s = jnp.where(valid, _nt_dot(q4, kw), NEG_INF)
        mx = jnp.max(s, axis=-1, keepdims=True)
        p = jnp.exp(s - mx)
        l = jnp.sum(p, axis=-1, keepdims=True)
        o = _unstack_heads(jnp.dot(p.astype(BF16), vw, preferred_element_type=F32) / l, masks)
        lse = _unstack_heads(jnp.broadcast_to(mx + jnp.log(l), (DIL_HEADS * tq, width)), masks)
        if dil > 1:
            for lc in range(nlc):
                ostage_scr[lc, pl.ds(row0 + r, tq, stride=dil), :] = o[:, lc * LANES:(lc + 1) * LANES]
                lstage_scr[lc, pl.ds(row0 + r, tq, stride=dil), :] = lse[:, lc * LANES:(lc + 1) * LANES]
        else:
            o_ref[row0:row0 + tq, :] = o
            lse_ref[row0:row0 + tq, :] = lse

    def residue(r):
        for u in range(nqb):
            block(r, u)

    if dil == 1:
        residue(0)
    else:
        def body(r, carry):
            residue(r)
            return carry

        lax.fori_loop(0, dil, body, 0, unroll=2)
        for lc in range(nlc):
            o_ref[:, lc * LANES:(lc + 1) * LANES] = ostage_scr[lc]
            lse_ref[:, lc * LANES:(lc + 1) * LANES] = lstage_scr[lc]


def _dil_attn(z, g, bsz, seq):
    dil = DIL_DILATIONS[g]
    n = z.shape[0]
    width = DIL_HEADS * HEAD_DIM
    seg = seq // dil
    tq = min(128, seg)
    win = min(2 * DIL_SIDE + tq, seg)
    nqb = max(1, min(4, seg // tq, DIL_MAX_STEP_ROWS // (tq * dil)))
    nq = seg // (tq * nqb)
    rows = nqb * tq * dil
    cq, ck, cv = (C_QD // width + g, C_KD // width + g, C_VD // width + g)
    scratch = []
    if dil > 1:
        nlc = width // LANES
        scratch = [pltpu.VMEM((seq, width), BF16), pltpu.VMEM((seq, width), BF16),
                   pltpu.VMEM((nlc, seq, LANES), F32),
                   pltpu.VMEM((nlc, rows, LANES), F32), pltpu.VMEM((nlc, rows, LANES), F32)]
    return pl.pallas_call(
        functools.partial(_dil_kernel, dil=dil, seg=seg, win=win, tq=tq, nqb=nqb),
        grid=(bsz, nq),
        in_specs=[pl.BlockSpec((rows, width), lambda b, i: (b * nq + i, cq)),
                  pl.BlockSpec((seq, width), lambda b, i: (b, ck)),
                  pl.BlockSpec((seq, width), lambda b, i: (b, cv))],
        out_specs=[pl.BlockSpec((rows, width), lambda b, i: (b * nq + i, 0)),
                   pl.BlockSpec((rows, width), lambda b, i: (b * nq + i, 0))],
        out_shape=[jax.ShapeDtypeStruct((n, width), F32), jax.ShapeDtypeStruct((n, width), F32)],
        scratch_shapes=scratch,
        compiler_params=_params("arbitrary", "arbitrary"),
        name=f"dil_attn_g{g}",
    )(z, z, z)


def _mix_out_kernel(x_ref, oa_ref, on_ref, od0_ref, od1_ref, od2_ref, l0_ref, l1_ref, l2_ref,
                    ga_ref, gn_ref, gd_ref, wpa_ref, wpn_ref, wpd_ref, wout_ref,
                    gm_ref, fg_ref, fsc_ref, fsh_ref, wr_ref, br_ref,
                    xo_ref, h2_ref, lg_ref, *, sub):
    def sig(g):
        return 1.0 / (1.0 + jnp.exp(-g.astype(F32)))

    for r0 in range(0, x_ref.shape[0], sub):
        rows = slice(r0, r0 + sub)
        l0, l1, l2 = l0_ref[rows, :], l1_ref[rows, :], l2_ref[rows, :]
        mx = jnp.maximum(jnp.maximum(l0, l1), l2)
        e0, e1, e2 = jnp.exp(l0 - mx), jnp.exp(l1 - mx), jnp.exp(l2 - mx)
        den = e0 + e1 + e2
        od = (e0 / den) * od0_ref[rows, :] + (e1 / den) * od1_ref[rows, :] + (e2 / den) * od2_ref[rows, :]
        merged = (sig(ga_ref[rows, :]) * jnp.dot(oa_ref[rows, :], wpa_ref[...], preferred_element_type=F32)
                  + sig(gn_ref[rows, :]) * jnp.dot(on_ref[rows, :], wpn_ref[...], preferred_element_type=F32)
                  + sig(gd_ref[rows, :]) * jnp.dot(od.astype(BF16), wpd_ref[...], preferred_element_type=F32))
        y = jnp.dot(merged.astype(BF16), wout_ref[...], preferred_element_type=F32)
        xn = x_ref[rows, :] + gm_ref[...] * y
        xo_ref[rows, :] = xn
        ms = jnp.mean(xn * xn, axis=-1, keepdims=True)
        h2 = xn * lax.rsqrt(ms + NORM_EPS) * fg_ref[...]
        h2 = h2 * (1.0 + fsc_ref[...]) + fsh_ref[...]
        lg_ref[:, rows] = lax.dot_general(wr_ref[...], h2, (((1,), (1,)), ((), ())), preferred_element_type=F32,
                                          precision=HIGHEST) + br_ref[...]
        for c in range(h2.shape[1] // LANES):
            h2_ref[pl.ds(r0 * SUBLANES + c, sub, stride=SUBLANES), :] = h2[:, c * LANES:(c + 1) * LANES]


def _mix_out(x2, z, o_a, o_n, o_d, lse_d, wpa, wpn, wpd, wout, gate_m, fg, fsc, fsh, wr, br, seq, tm=512, sub=256):
    n, d = x2.shape
    assert d // LANES == SUBLANES and C_GATE % d == 0
    spb = seq // tm
    gcol = C_GATE // d
    wd = DIL_HEADS * HEAD_DIM
    row = lambda w: pl.BlockSpec((tm, w), lambda i: (i, 0))
    full = lambda a: pl.BlockSpec(a.shape, lambda i: (0,) * a.ndim)
    perb = pl.BlockSpec((None, 1, d), lambda i: (i // spb, 0, 0))
    return pl.pallas_call(
        functools.partial(_mix_out_kernel, sub=min(sub, tm)),
        grid=(n // tm,),
        in_specs=[row(d), row(o_a.shape[1]), row(o_n.shape[1]), row(wd), row(wd), row(wd), row(wd), row(wd), row(wd),
                  pl.BlockSpec((tm, d), lambda i: (i, gcol)),
                  pl.BlockSpec((tm, d), lambda i: (i, gcol + 1)),
                  pl.BlockSpec((tm, d), lambda i: (i, gcol + 2)),
                  full(wpa), full(wpn), full(wpd), full(wout),
                  perb, full(fg), perb, perb, full(wr), full(br)],
        out_specs=[row(d), pl.BlockSpec((tm * SUBLANES, LANES), lambda i: (i, 0)),
                   pl.BlockSpec((wr.shape[0], tm), lambda i: (0, i))],
        out_shape=[jax.ShapeDtypeStruct((n, d), F32),
                   jax.ShapeDtypeStruct((n * SUBLANES, LANES), F32),
                   jax.ShapeDtypeStruct((wr.shape[0], n), F32)],
        compiler_params=_params("arbitrary"),
        name="mix_out",
    )(x2, o_a, o_n, o_d[0], o_d[1], o_d[2], lse_d[0], lse_d[1], lse_d[2], z, z, z,
      wpa, wpn, wpd, wout, gate_m, fg, fsc, fsh, wr, br)


def _route_kernel(lg_ref, e_ref, w_ref, cnt_ref):
    i = pl.program_id(0)
    le = lg_ref[0:N_EXPERTS, :]
    gl = lg_ref[N_EXPERTS:N_EXPERTS + N_GROUPS, :]
    tb = le.shape[1]
    gmax = jnp.max(gl, axis=0, keepdims=True)
    grow = lax.broadcasted_iota(I32, gl.shape, 0)
    gidx = jnp.min(jnp.where(gl == gmax, grow, N_GROUPS), axis=0, keepdims=True)
    pg_top = 1.0 / jnp.sum(jnp.exp(gl - gmax), axis=0, keepdims=True)
    erow = lax.broadcasted_iota(I32, le.shape, 0)
    lm = jnp.where((erow // EXPERTS_PER_GROUP) == gidx, le, -jnp.inf)
    m1 = jnp.max(lm, axis=0, keepdims=True)
    i1 = jnp.min(jnp.where(lm == m1, erow, N_EXPERTS), axis=0, keepdims=True)
    lm2 = jnp.where(erow == i1, -jnp.inf, lm)
    m2 = jnp.max(lm2, axis=0, keepdims=True)
    i2 = jnp.min(jnp.where(lm2 == m2, erow, N_EXPERTS), axis=0, keepdims=True)
    t = jnp.exp(m2 - m1)
    e_ref[0:1, :] = i1
    e_ref[1:2, :] = i2
    w_ref[0:1, :] = pg_top / (1.0 + t)
    w_ref[1:2, :] = pg_top * t / (1.0 + t)
    oh = (erow == i1).astype(F32) + (erow == i2).astype(F32)
    cnt = jnp.sum(oh, axis=1, keepdims=True)

    @pl.when(i == 0)
    def _():
        cnt_ref[...] = jnp.zeros_like(cnt_ref)

    cnt_ref[...] += jnp.broadcast_to(cnt, cnt_ref.shape)


def _route(lgt, tb=1024):
    rows, n = lgt.shape
    tb = min(tb, n)
    return pl.pallas_call(
        _route_kernel,
        grid=(n // tb,),
        in_specs=[pl.BlockSpec((rows, tb), lambda i: (0, i))],
        out_specs=[pl.BlockSpec((2, tb), lambda i: (0, i)),
                   pl.BlockSpec((2, tb), lambda i: (0, i)),
                   pl.BlockSpec((N_EXPERTS, LANES), lambda i: (0, 0))],
        out_shape=[jax.ShapeDtypeStruct((2, n), I32),
                   jax.ShapeDtypeStruct((2, n), F32),
                   jax.ShapeDtypeStruct((N_EXPERTS, LANES), F32)],
        compiler_params=_params("arbitrary"),
        name="route",
    )(lgt)


def _rank_kernel(e_ref, cnt_ref, dest_ref, be_ref, nu_ref, zs_ref, carry_scr, base_scr, *, blk):
    i = pl.program_id(0)

    @pl.when(i == 0)
    def _():
        cnt = cnt_ref[...]
        padded = jnp.floor((cnt + (blk - 1)) / blk) * blk
        r = lax.broadcasted_iota(I32, cnt.shape, 0)
        c = lax.broadcasted_iota(I32, cnt.shape, 1)
        prow = jnp.sum(jnp.where(r == c, padded, 0.0), axis=0, keepdims=True)
        pad_end = jnp.sum(jnp.where(c <= r, prow, 0.0), axis=1, keepdims=True)
        base = jnp.broadcast_to(pad_end, cnt.shape) - padded
        base_scr[...] = base
        carry_scr[...] = jnp.zeros_like(carry_scr)
        zs_ref[...] = jnp.sum(jnp.where(r == c, base + cnt, 0.0) + jnp.where(r + N_EXPERTS == c, padded - cnt, 0.0),
                              axis=0, keepdims=True).astype(I32)
        jb = lax.broadcasted_iota(I32, (N_EXPERTS, be_ref.shape[1]), 1).astype(F32) * blk
        be = jnp.sum((pad_end <= jb).astype(I32), axis=0, keepdims=True)
        be_ref[...] = jnp.minimum(be, N_EXPERTS - 1)
        nu_ref[...] = jnp.broadcast_to(jnp.max(pad_end, axis=0, keepdims=True) / blk, nu_ref.shape).astype(I32)

    e = e_ref[...]
    tb = e.shape[1]
    erow = lax.broadcasted_iota(I32, (N_EXPERTS, tb), 0)
    oh0 = erow == e[0:1]
    oh1 = erow == e[1:2]
    both = jnp.where(oh0 | oh1, 1.0, 0.0)
    upper = jnp.where(lax.broadcasted_iota(I32, (tb, tb), 0) < lax.broadcasted_iota(I32, (tb, tb), 1), 1.0, 0.0)
    prefix = jnp.dot(both.astype(BF16), upper.astype(BF16), preferred_element_type=F32)
    tot = prefix + base_scr[:, 0:1] + carry_scr[:, 0:1]
    dest_ref[0:1, :] = jnp.sum(jnp.where(oh0, tot, 0.0), axis=0, keepdims=True).astype(I32)
    dest_ref[1:2, :] = jnp.sum(jnp.where(oh1, tot, 0.0), axis=0, keepdims=True).astype(I32)
    carry_scr[...] += jnp.broadcast_to(jnp.sum(both, axis=1, keepdims=True), carry_scr.shape)


def _rank(eidx, cnt, n_blocks, blk, tb=512):
    n = eidx.shape[1]
    tb = min(tb, n)
    nbp = -(-n_blocks // LANES) * LANES
    return pl.pallas_call(
        functools.partial(_rank_kernel, blk=blk),
        grid=(n // tb,),
        in_specs=[pl.BlockSpec((2, tb), lambda i: (0, i)),
                  pl.BlockSpec(cnt.shape, lambda i: (0, 0))],
        out_specs=[pl.BlockSpec((2, tb), lambda i: (0, i)),
                   pl.BlockSpec((1, nbp), lambda i: (0, 0)),
                   pl.BlockSpec((1, LANES), lambda i: (0, 0)),
                   pl.BlockSpec((1, LANES), lambda i: (0, 0))],
        out_shape=[jax.ShapeDtypeStruct((2, n), I32),
                   jax.ShapeDtypeStruct((1, nbp), I32),
                   jax.ShapeDtypeStruct((1, LANES), I32),
                   jax.ShapeDtypeStruct((1, LANES), I32)],
        scratch_shapes=[pltpu.VMEM((N_EXPERTS, LANES), F32), pltpu.VMEM((N_EXPERTS, LANES), F32)],
        compiler_params=_params("arbitrary"),
        name="rank",
    )(eidx, cnt)


def _tile_copy(src, src_row, dst, dst_row, sem):
    return pltpu.make_async_copy(src.at[pl.ds(pl.multiple_of(src_row * SUBLANES, SUBLANES), SUBLANES), :],
                                 dst.at[pl.ds(pl.multiple_of(dst_row * SUBLANES, SUBLANES), SUBLANES), :], sem)


def _dispatch_kernel(zs_ref, nu_ref, dest_ref, h_ref, xs_hbm, zero_scr, sem, zsem, *, tb, blk, total_blocks):
    def zero_fill(slot, nslots):
        start = pl.multiple_of(slot * SUBLANES, SUBLANES)
        return pltpu.make_async_copy(zero_scr.at[pl.ds(0, nslots * SUBLANES), :],
                                     xs_hbm.at[pl.ds(start, nslots * SUBLANES), :], zsem)

    def pad_fills(act):
        for e in range(N_EXPERTS):
            off = zs_ref[e]
            count = zs_ref[N_EXPERTS + e]
            for b in range(blk.bit_length() - 1):
                bit = (count >> b) & 1

                @pl.when(bit == 1)
                def _(off=off, b=b):
                    act(zero_fill(off, 1 << b))

                off = off + bit * (1 << b)

    @pl.when(pl.program_id(0) == 0)
    def _():
        zero_scr[...] = jnp.zeros_like(zero_scr)
        pad_fills(lambda cp: cp.start())

        def tail_start(j, carry):
            zero_fill(j * blk, blk).start()
            return carry

        def tail_wait(j, carry):
            zero_fill(j * blk, blk).wait()
            return carry

        lax.fori_loop(nu_ref[0], total_blocks, tail_start, 0)
        pad_fills(lambda cp: cp.wait())
        lax.fori_loop(nu_ref[0], total_blocks, tail_wait, 0)

    def issue(t, carry):
        for k in range(2):
            _tile_copy(h_ref, t, xs_hbm, dest_ref[k, t], sem).start()
        return carry

    lax.fori_loop(0, tb, issue, 0)
    for k in range(2):
        pltpu.make_async_copy(h_ref, xs_hbm.at[pl.ds(0, tb * SUBLANES), :], sem).wait()


def _dispatch(zstart, n_used, dest, h2t, n_blocks, blk, tb=512):
    n = dest.shape[1]
    tb = min(tb, n)
    grid_spec = pltpu.PrefetchScalarGridSpec(
        num_scalar_prefetch=2,
        grid=(n // tb,),
        in_specs=[pl.BlockSpec((2, tb), lambda i, zs, nu: (0, i), memory_space=pltpu.SMEM),
                  pl.BlockSpec((tb * SUBLANES, LANES), lambda i, zs, nu: (i, 0))],
        out_specs=pl.BlockSpec(memory_space=pl.ANY),
        scratch_shapes=[pltpu.VMEM((blk * SUBLANES, LANES), F32), pltpu.SemaphoreType.DMA, pltpu.SemaphoreType.DMA],
    )
    total_blocks = n_blocks
    return pl.pallas_call(
        functools.partial(_dispatch_kernel, tb=tb, blk=blk, total_blocks=total_blocks),
        grid_spec=grid_spec,
        out_shape=jax.ShapeDtypeStruct((total_blocks * blk * SUBLANES, LANES), F32),
        compiler_params=pltpu.CompilerParams(dimension_semantics=("arbitrary",), has_side_effects=True),
        name="dispatch",
    )(zstart, n_used, dest, h2t)


def _expert_kernel(be_ref, first_ref, nu_ref, xs_ref, w1_ref, w3_ref, w2_ref, y_ref, xb_scr, w1_scr, w3_scr, w2_scr, *, blk):
    del be_ref
    j = pl.program_id(0)
    nchunk = xb_scr.shape[1] // LANES

    @pl.when(j < nu_ref[0])
    def _():
        @pl.when(first_ref[j] == 1)
        def _():
            w1_scr[...] = w1_ref[...].astype(BF16)
            w3_scr[...] = w3_ref[...].astype(BF16)
            w2_scr[...] = w2_ref[...].astype(BF16)

        for c in range(nchunk):
            xb_scr[:, c * LANES:(c + 1) * LANES] = xs_ref[pl.ds(c, blk, stride=SUBLANES), :].astype(BF16)
        xb = xb_scr[...]
        a = jnp.dot(xb, w1_scr[...], preferred_element_type=F32)
        b = jnp.dot(xb, w3_scr[...], preferred_element_type=F32)
        hid = (a / (1.0 + jnp.exp(-a))) * b
        y = jnp.dot(hid.astype(BF16), w2_scr[...], preferred_element_type=F32)
        for c in range(nchunk):
            y_ref[pl.ds(c, blk, stride=SUBLANES), :] = y[:, c * LANES:(c + 1) * LANES]

    @pl.when(j >= nu_ref[0])
    def _():
        y_ref[...] = jnp.zeros_like(y_ref)


def _experts(block_e, first, n_used, xs, w1, w3, w2, layer, n_blocks, blk):
    d, hid = w1.shape[2], w1.shape[3]
    used = lambda j, nu: jnp.minimum(j, nu[0] - 1)
    grid_spec = pltpu.PrefetchScalarGridSpec(
        num_scalar_prefetch=3,
        grid=(n_blocks,),
        in_specs=[pl.BlockSpec((blk * SUBLANES, LANES), lambda j, be, fi, nu: (used(j, nu), 0)),
                  pl.BlockSpec((None, None, d, hid), lambda j, be, fi, nu: (layer, be[j], 0, 0)),
                  pl.BlockSpec((None, None, d, hid), lambda j, be, fi, nu: (layer, be[j], 0, 0)),
                  pl.BlockSpec((None, None, hid, d), lambda j, be, fi, nu: (layer, be[j], 0, 0))],
        out_specs=pl.BlockSpec((blk * SUBLANES, LANES), lambda j, be, fi, nu: (j, 0)),
        scratch_shapes=[pltpu.VMEM((blk, d), BF16), pltpu.VMEM((d, hid), BF16), pltpu.VMEM((d, hid), BF16),
                        pltpu.VMEM((hid, d), BF16)],
    )
    return pl.pallas_call(
        functools.partial(_expert_kernel, blk=blk),
        grid_spec=grid_spec,
        out_shape=jax.ShapeDtypeStruct((n_blocks * blk * SUBLANES, LANES), F32),
        compiler_params=_params("arbitrary"),
        name="experts",
    )(block_e, first, n_used, xs, w1, w3, w2)


def _combine_kernel(dcur_ref, dnext_ref, x_ref, w_ref, gf_ref, fg_ref, y_hbm, xo_ref, g_scr, sems, *, tb, nsteps):
    i = pl.program_id(0)
    slot = i % 2

    def gather(d_ref, s):
        def issue(t, carry):
            for k in range(2):
                _tile_copy(y_hbm, d_ref[k, t], g_scr.at[s, k], t, sems.at[s]).start()
            return carry

        lax.fori_loop(0, tb, issue, 0)

    @pl.when(i == 0)
    def _():
        gather(dcur_ref, 0)

    @pl.when(i + 1 < nsteps)
    def _():
        gather(dnext_ref, 1 - slot)

    for k in range(2):
        pltpu.make_async_copy(y_hbm.at[pl.ds(0, tb * SUBLANES), :], g_scr.at[slot, k], sems.at[slot]).wait()
    w0 = w_ref[:, 0:1]
    w1 = w_ref[:, 1:2]
    for c in range(x_ref.shape[1] // LANES):
        sl = slice(c * LANES, (c + 1) * LANES)
        yc = (w0 * g_scr[slot, 0, pl.ds(c, tb, stride=SUBLANES), :]
              + w1 * g_scr[slot, 1, pl.ds(c, tb, stride=SUBLANES), :])
        xo_ref[:, sl] = x_ref[:, sl] + gf_ref[:, sl] * yc
    xn = xo_ref[...]
    ms = jnp.mean(xn * xn, axis=-1, keepdims=True)
    xo_ref[...] = xn * lax.rsqrt(ms + NORM_EPS) * fg_ref[...]


def _combine_final(dest, x2, wts_t, gate_f, y, final_g, seq, tb=256):
    n, d = x2.shape
    spb = seq // tb
    nsteps = n // tb
    return pl.pallas_call(
        functools.partial(_combine_kernel, tb=tb, nsteps=nsteps),
        grid=(nsteps,),
        in_specs=[pl.BlockSpec((2, tb), lambda i: (0, i), memory_space=pltpu.SMEM),
                  pl.BlockSpec((2, tb), lambda i: (0, jnp.minimum(i + 1, nsteps - 1)), memory_space=pltpu.SMEM),
                  pl.BlockSpec((tb, d), lambda i: (i, 0)),
                  pl.BlockSpec((tb, 2), lambda i: (i, 0)),
                  pl.BlockSpec((None, 1, d), lambda i: (i // spb, 0, 0)),
                  pl.BlockSpec((1, d), lambda i: (0, 0)),
                  pl.BlockSpec(memory_space=pl.ANY)],
        out_specs=pl.BlockSpec((tb, d), lambda i: (i, 0)),
        out_shape=jax.ShapeDtypeStruct((n, d), F32),
        scratch_shapes=[pltpu.VMEM((2, 2, tb * SUBLANES, LANES), F32), pltpu.SemaphoreType.DMA((2,))],
        compiler_params=_params("arbitrary"),
        name="combine_final",
    )(dest, dest, x2, wts_t, gate_f, final_g, y)


def _rotary_tables(seq, width):
    inv = 1.0 / (ROPE_THETA ** (jnp.arange(0, HEAD_DIM, 2, dtype=F32) / HEAD_DIM))
    ang = jnp.arange(seq, dtype=F32)[:, None] * inv[None, :]
    ang = jnp.concatenate([ang, ang], axis=-1)
    sign = jnp.concatenate([-jnp.ones((HEAD_DIM // 2,), F32), jnp.ones((HEAD_DIM // 2,), F32)])
    reps = width // HEAD_DIM
    return jnp.tile(jnp.cos(ang), (1, reps)), jnp.tile(jnp.sin(ang) * sign[None, :], (1, reps))


def _col_scale():
    colscale = jnp.ones((1, NC), F32)
    for lo, hi in ((C_QA, C_KA), (C_QD, C_KD), (C_QN, C_KN)):
        colscale = colscale.at[:, lo:hi].set(HEAD_DIM ** -0.5)
    return colscale.at[:, C_QA:C_KA].multiply(math.log2(math.e))


def kernel(x, c, ada_w, ada_b, mix_norm_g, ffn_norm_g, w_in, da_lambda, da_subln_g, na_rpb, w_proj_a, w_proj_n, w_proj_d, w_out, router_group_w, router_group_b, router_expert_w, router_expert_b, expert_w1, expert_w3, expert_w2, final_norm_g):
    bsz, seq, d = x.shape
    depth = ada_w.shape[0]
    n = bsz * seq
    tn = 512
    assert w_in.shape[2] == NC

    mods = _ada(c, ada_w, ada_b).reshape(depth, bsz, 6, 1, d)
    cos_t, sin_t = _rotary_tables(seq, LANES)
    colscale = _col_scale()

    n_blocks = (2 * n) // MOE_BLK + N_EXPERTS
    rpad = SUBLANES - N_GROUPS

    x2 = x.reshape(n, d)
    moe = None
    for l in range(depth):
        lam_init = 0.8 - 0.6 * math.exp(-0.3 * l)
        shift_m, scale_m, gate_m, shift_f, scale_f, gate_f = (mods[l, :, k] for k in range(6))
        w_bf = jnp.concatenate([w_in[l, :, lo:hi] for lo, hi in IN_PERM], axis=1).astype(BF16)
        proj_args = (mix_norm_g[l].reshape(1, d), scale_m, shift_m, cos_t, sin_t, colscale, w_bf, seq)
        if moe is None:
            z = _inproj(x2, *proj_args, tn=tn)
        else:
            x2, z = _combine_inproj(moe[0], x2, *moe[1:], *proj_args, tn=tn)
        o_a = _diff_attn(z, da_lambda[l], da_subln_g[l].reshape(1, 2 * HEAD_DIM), lam_init, bsz, seq)
        o_n = _na_attn(z, _na_bias_table(na_rpb[l]), bsz, seq)
        dil = [_dil_attn(z, g, bsz, seq) for g in range(len(DIL_DILATIONS))]
        wr = jnp.concatenate([router_expert_w[l], router_group_w[l], jnp.zeros((d, rpad), F32)], axis=1).T
        br = jnp.concatenate([router_expert_b[l], router_group_b[l], jnp.zeros((rpad,), F32)]).reshape(-1, 1)
        x2, h2t, lgt = _mix_out(
            x2, z, o_a, o_n, [o for o, _ in dil], [s for _, s in dil],
            w_proj_a[l].astype(BF16), w_proj_n[l].astype(BF16), w_proj_d[l].astype(BF16), w_out[l].astype(BF16),
            gate_m, ffn_norm_g[l].reshape(1, d), scale_f, shift_f, wr, br, seq)
        eidx, wts, cnt = _route(lgt)
        dest, block_e, n_used, zstart = _rank(eidx, cnt, n_blocks, MOE_BLK)
        block_e = block_e[0, :n_blocks]
        first = jnp.concatenate([jnp.ones((1,), I32), (block_e[1:] != block_e[:-1]).astype(I32)])
        n_used = n_used[0, :1]
        xs = _dispatch(zstart[0, :2 * N_EXPERTS], n_used, dest, h2t, n_blocks, MOE_BLK)
        y = _experts(block_e, first, n_used, xs, expert_w1, expert_w3, expert_w2, l, n_blocks, MOE_BLK)
        moe = (dest, wts.T, gate_f, y)
    return _combine_final(moe[0], x2, *moe[1:], final_norm_g.reshape(1, d), seq).reshape(bsz, seq, d)
```

```python
import functools
import math

import jax
import jax.numpy as jnp
from jax import lax
from jax.experimental import pallas as pl
from jax.experimental.pallas import tpu as pltpu

F32 = jnp.float32
BF16 = jnp.bfloat16
I32 = jnp.int32
HIGHEST = lax.Precision.HIGHEST

HEAD_DIM = 64
ROPE_THETA = 10000.0
NORM_EPS = 1e-6
NEG_INF = -1e30

DA_HEADS = 4
GRID_W = 64
NA_HEADS = 4
NA_WIN_ROWS = 8
NA_WIN_COLS = 16
DIL_DILATIONS = (1, 4, 16)
DIL_SIDE = 64
DIL_HEADS = 4
DIL_MAX_STEP_ROWS = 2048
N_GROUPS = 4
EXPERTS_PER_GROUP = 8
N_EXPERTS = N_GROUPS * EXPERTS_PER_GROUP

LANES = 128
SUBLANES = 8
MOE_BLK = 512
VMEM_LIMIT = 56 * 1024 * 1024

C_QA, C_KA, C_QD, C_KD = 0, 512, 1024, 1792
N_ROT = 2560
C_VA, C_GATE, C_QN, C_KN, C_VN, C_VD = 2560, 3072, 6144, 6400, 6656, 6912
NC = 7680
IN_PERM = ((0, 1024), (2304, 3840), (1024, 1536), (4608, 7680), (1536, 2304), (3840, 4608))


def _nt_dot(a, b):
    return lax.dot_general(a, b, (((1,), (1,)), ((), ())), preferred_element_type=F32)


def _stack_heads(q, nh):
    lane = lax.broadcasted_iota(I32, q.shape, 1)
    masks = [(lane >= HEAD_DIM * h) & (lane < HEAD_DIM * (h + 1)) for h in range(nh)]
    return jnp.concatenate([jnp.where(hm, q, jnp.zeros_like(q)) for hm in masks], axis=0), masks


def _unstack_heads(x, masks):
    m = x.shape[0] // len(masks)
    out = x[0:m]
    for h in range(1, len(masks)):
        out = jnp.where(masks[h], x[h * m:(h + 1) * m], out)
    return out


def _params(*sem, unchecked_indices=False):
    return pltpu.CompilerParams(dimension_semantics=sem, vmem_limit_bytes=VMEM_LIMIT,
                                disable_bounds_checks=unchecked_indices)


def _ada_kernel(c_ref, w_ref, b_ref, o_ref):
    c = c_ref[...]
    ca = c / (1.0 + jnp.exp(-c))
    o_ref[...] = jnp.dot(ca, w_ref[...], preferred_element_type=F32, precision=HIGHEST) + b_ref[...]


def _ada(c, ada_w, ada_b):
    depth, d, w6 = ada_w.shape
    bsz = c.shape[0]
    rows = -(-bsz // SUBLANES) * SUBLANES
    cp = jnp.zeros((rows, d), F32).at[:bsz].set(c)
    tn = w6 // 4
    out = pl.pallas_call(
        _ada_kernel,
        grid=(depth, w6 // tn),
        in_specs=[pl.BlockSpec((rows, d), lambda l, j: (0, 0)),
                  pl.BlockSpec((None, d, tn), lambda l, j: (l, 0, j)),
                  pl.BlockSpec((None, 1, tn), lambda l, j: (l, 0, j))],
        out_specs=pl.BlockSpec((None, rows, tn), lambda l, j: (l, 0, j)),
        out_shape=jax.ShapeDtypeStruct((depth, rows, w6), F32),
        compiler_params=_params("arbitrary", "arbitrary"),
        name="ada_mod",
    )(cp, ada_w, ada_b.reshape(depth, 1, w6))
    return out[:, :bsz]


def _norm_modulate(x, g_ref, sc_ref, sh_ref):
    ms = jnp.mean(x * x, axis=-1, keepdims=True)
    y = x * lax.rsqrt(ms + NORM_EPS) * g_ref[...]
    return y * (1.0 + sc_ref[...]) + sh_ref[...]


def _project_chunk(j, tn, h_scr, w_scr, cs_ref, cos_ref, sin_ref, z_ref):
    cols = slice(j * tn, (j + 1) * tn)
    acc = jnp.dot(h_scr[...], w_scr[:, cols], preferred_element_type=F32) * cs_ref[:, cols]
    if j < N_ROT // tn:
        half = HEAD_DIM // 2
        lane = lax.broadcasted_iota(I32, (acc.shape[0], LANES), 1)
        lo = (lane % HEAD_DIM) < half
        cos = cos_ref[...]
        sin = sin_ref[...]
        for cc in range(tn // LANES):
            a = acc[:, cc * LANES:(cc + 1) * LANES]
            rot = jnp.where(lo, pltpu.roll(a, LANES - half, 1), pltpu.roll(a, half, 1))
            c0 = j * tn + cc * LANES
            z_ref[:, c0:c0 + LANES] = (a * cos + rot * sin).astype(BF16)
    else:
        z_ref[:, cols] = acc.astype(BF16)


def _inproj_kernel(x_ref, g_ref, sc_ref, sh_ref, cos_ref, sin_ref, cs_ref, w_hbm, z_ref, h_scr, w_scr, sem, *, tn):
    @pl.when(pl.program_id(0) == 0)
    def _():
        cp = pltpu.make_async_copy(w_hbm, w_scr, sem)
        cp.start()
        cp.wait()

    h_scr[...] = _norm_modulate(x_ref[...], g_ref, sc_ref, sh_ref).astype(BF16)
    for j in range(w_scr.shape[1] // tn):
        _project_chunk(j, tn, h_scr, w_scr, cs_ref, cos_ref, sin_ref, z_ref)


def _combine_inproj_kernel(dcur_ref, dnext_ref, x_ref, wt_ref, gf_ref, y_hbm,
                           g_ref, sc_ref, sh_ref, cos_ref, sin_ref, cs_ref, w_hbm,
                           xo_ref, z_ref, h_scr, w_scr, g_scr, wsem, gsem, *, tn, tb, nsteps):
    i = pl.program_id(0)

    @pl.when(i == 0)
    def _():
        cp = pltpu.make_async_copy(w_hbm, w_scr, wsem)
        cp.start()

        def issue(t, carry):
            for k in range(2):
                _tile_copy(y_hbm, dcur_ref[k, t], g_scr.at[k], t, gsem).start()
            return carry

        lax.fori_loop(0, tb, issue, 0)
        cp.wait()

    for k in range(2):
        pltpu.make_async_copy(y_hbm.at[pl.ds(0, tb * SUBLANES), :], g_scr.at[k], gsem).wait()
    w0 = wt_ref[:, 0:1]
    w1 = wt_ref[:, 1:2]
    for c in range(x_ref.shape[1] // LANES):
        sl = slice(c * LANES, (c + 1) * LANES)
        yc = w0 * g_scr[0, pl.ds(c, tb, stride=SUBLANES), :] + w1 * g_scr[1, pl.ds(c, tb, stride=SUBLANES), :]
        xo_ref[:, sl] = x_ref[:, sl] + gf_ref[:, sl] * yc
    h_scr[...] = _norm_modulate(xo_ref[...], g_ref, sc_ref, sh_ref).astype(BF16)

    nchunks = w_scr.shape[1] // tn
    per = -(-tb // nchunks)
    for j in range(nchunks):
        _project_chunk(j, tn, h_scr, w_scr, cs_ref, cos_ref, sin_ref, z_ref)
        for t in range(j * per, min((j + 1) * per, tb)):
            for k in range(2):
                _tile_copy(y_hbm, dnext_ref[k, t], g_scr.at[k], t, gsem).start()

    @pl.when(i == nsteps - 1)
    def _():
        for k in range(2):
            pltpu.make_async_copy(y_hbm.at[pl.ds(0, tb * SUBLANES), :], g_scr.at[k], gsem).wait()


def _combine_inproj(dest, x2, wts_t, gate_f, y, g, scale, shift, cos_t, sin_t, colscale, w_bf, seq, tm=512, tn=512):
    n, d = x2.shape
    nc = w_bf.shape[1]
    assert N_ROT % tn == 0 and nc % tn == 0
    spb = seq // tm
    nsteps = n // tm
    perb = pl.BlockSpec((None, 1, d), lambda i: (i // spb, 0, 0))
    return pl.pallas_call(
        functools.partial(_combine_inproj_kernel, tn=tn, tb=tm, nsteps=nsteps),
        grid=(nsteps,),
        in_specs=[pl.BlockSpec((2, tm), lambda i: (0, i), memory_space=pltpu.SMEM),
                  pl.BlockSpec((2, tm), lambda i: (0, jnp.minimum(i + 1, nsteps - 1)), memory_space=pltpu.SMEM),
                  pl.BlockSpec((tm, d), lambda i: (i, 0)),
                  pl.BlockSpec((tm, 2), lambda i: (i, 0)),
                  perb,
                  pl.BlockSpec(memory_space=pl.ANY),
                  pl.BlockSpec((1, d), lambda i: (0, 0)),
                  perb, perb,
                  pl.BlockSpec((tm, LANES), lambda i: (i % spb, 0)),
                  pl.BlockSpec((tm, LANES), lambda i: (i % spb, 0)),
                  pl.BlockSpec((1, nc), lambda i: (0, 0)),
                  pl.BlockSpec(memory_space=pl.ANY)],
        out_specs=[pl.BlockSpec((tm, d), lambda i: (i, 0)), pl.BlockSpec((tm, nc), lambda i: (i, 0))],
        out_shape=[jax.ShapeDtypeStruct((n, d), F32), jax.ShapeDtypeStruct((n, nc), BF16)],
        scratch_shapes=[pltpu.VMEM((tm, d), BF16), pltpu.VMEM((d, nc), BF16),
                        pltpu.VMEM((2, tm * SUBLANES, LANES), F32),
                        pltpu.SemaphoreType.DMA, pltpu.SemaphoreType.DMA],
        compiler_params=_params("arbitrary"),
        name="combine_inproj",
    )(dest, dest, x2, wts_t, gate_f, y, g, scale, shift, cos_t, sin_t, colscale, w_bf)


def _inproj(x2, g, scale, shift, cos_t, sin_t, colscale, w_bf, seq, tm=512, tn=512):
    n, d = x2.shape
    nc = w_bf.shape[1]
    assert N_ROT % tn == 0 and nc % tn == 0
    spb = seq // tm
    return pl.pallas_call(
        functools.partial(_inproj_kernel, tn=tn),
        grid=(n // tm,),
        in_specs=[pl.BlockSpec((tm, d), lambda i: (i, 0)),
                  pl.BlockSpec((1, d), lambda i: (0, 0)),
                  pl.BlockSpec((None, 1, d), lambda i: (i // spb, 0, 0)),
                  pl.BlockSpec((None, 1, d), lambda i: (i // spb, 0, 0)),
                  pl.BlockSpec((tm, LANES), lambda i: (i % spb, 0)),
                  pl.BlockSpec((tm, LANES), lambda i: (i % spb, 0)),
                  pl.BlockSpec((1, nc), lambda i: (0, 0)),
                  pl.BlockSpec(memory_space=pl.ANY)],
        out_specs=pl.BlockSpec((tm, nc), lambda i: (i, 0)),
        out_shape=jax.ShapeDtypeStruct((n, nc), BF16),
        scratch_shapes=[pltpu.VMEM((tm, d), BF16), pltpu.VMEM((d, nc), BF16), pltpu.SemaphoreType.DMA],
        compiler_params=_params("arbitrary"),
        name="inproj",
    )(x2, g, scale, shift, cos_t, sin_t, colscale, w_bf)


def _diff_attn_kernel(lam_ref, g_ref, q_ref, k_ref, v_ref, o_ref, *, lam_init, chunk):
    lp = lam_ref[...]
    lam = (jnp.exp(jnp.sum(lp[0:1] * lp[1:2], axis=-1, keepdims=True))
           - jnp.exp(jnp.sum(lp[2:3] * lp[3:4], axis=-1, keepdims=True)) + lam_init)
    k = k_ref[...]
    v = v_ref[...]
    lane = lax.broadcasted_iota(I32, (chunk, 2 * HEAD_DIM), 1)
    chains = [(c, m) for c in range(q_ref.shape[0] // chunk) for m in range(2)]

    def scores(c, m):
        q = q_ref[c * chunk:(c + 1) * chunk, :]
        qm = jnp.where((lane >= HEAD_DIM * m) & (lane < HEAD_DIM * (m + 1)), q, jnp.zeros_like(q))
        return _nt_dot(qm, k)

    s_next = scores(*chains[0])
    pb, ls = {}, {}
    for i, (c, m) in enumerate(chains):
        s = s_next
        if i + 1 < len(chains):
            s_next = scores(*chains[i + 1])
        mx = jnp.max(s, axis=-1, keepdims=True)
        p = jnp.exp2(s - mx)
        ls[m] = jnp.sum(p, axis=-1, keepdims=True)
        pb[m] = p.astype(BF16)
        if m == 1:
            ratio = (lam * ls[0] / ls[1]).astype(BF16)
            a = pb[0] - pb[1] * ratio
            o = jnp.dot(a, v, preferred_element_type=F32) / ls[0]
            ms = jnp.mean(o * o, axis=-1, keepdims=True)
            o = o * lax.rsqrt(ms + NORM_EPS) * g_ref[...] * (1.0 - lam_init)
            o_ref[c * chunk:(c + 1) * chunk, :] = o.astype(BF16)


def _diff_attn(z, lam_p, subln_g, lam_init, bsz, seq, tq=1024, chunk=256):
    n = z.shape[0]
    hw = 2 * HEAD_DIM
    nq = seq // tq
    return pl.pallas_call(
        functools.partial(_diff_attn_kernel, lam_init=lam_init, chunk=min(chunk, tq)),
        grid=(bsz, DA_HEADS, nq),
        in_specs=[pl.BlockSpec((4, HEAD_DIM), lambda b, h, i: (0, 0)),
                  pl.BlockSpec((1, hw), lambda b, h, i: (0, 0)),
                  pl.BlockSpec((tq, hw), lambda b, h, i: (b * nq + i, C_QA // hw + h)),
                  pl.BlockSpec((seq, hw), lambda b, h, i: (b, C_KA // hw + h)),
                  pl.BlockSpec((seq, hw), lambda b, h, i: (b, C_VA // hw + h))],
        out_specs=pl.BlockSpec((tq, hw), lambda b, h, i: (b * nq + i, h)),
        out_shape=jax.ShapeDtypeStruct((n, DA_HEADS * hw), BF16),
        compiler_params=_params("arbitrary", "arbitrary", "arbitrary"),
        name="diff_attn",
    )(lam_p, subln_g, z, z, z)


def _na_kernel(q_ref, k_ref, v_ref, bias_ref, o_ref, *, rows):
    win = NA_WIN_ROWS * GRID_W

    def body(r, carry):
        kr0 = jnp.clip(r - NA_WIN_ROWS // 2, 0, rows - NA_WIN_ROWS)
        pat = kr0 - r + (NA_WIN_ROWS - 1)
        q4, masks = _stack_heads(q_ref[pl.ds(pl.multiple_of(r * GRID_W, GRID_W), GRID_W), :], NA_HEADS)
        kw = k_ref[pl.ds(pl.multiple_of(kr0 * GRID_W, GRID_W), win), :]
        vw = v_ref[pl.ds(pl.multiple_of(kr0 * GRID_W, GRID_W), win), :]
        s = _nt_dot(q4, kw) + bias_ref[pat]
        mx = jnp.max(s, axis=-1, keepdims=True)
        p = jnp.exp(s - mx)
        l = jnp.sum(p, axis=-1, keepdims=True)
        o4 = jnp.dot(p.astype(BF16), vw, preferred_element_type=F32) / l
        o_ref[pl.ds(pl.multiple_of(r * GRID_W, GRID_W), GRID_W), :] = _unstack_heads(o4, masks).astype(BF16)
        return carry

    lax.fori_loop(0, rows, body, 0, unroll=4)


def _na_bias_table(rpb):
    c = jnp.arange(GRID_W)[:, None]
    kc = jnp.arange(GRID_W)[None, :]
    kc0 = jnp.clip(c - NA_WIN_COLS // 2, 0, GRID_W - NA_WIN_COLS)
    valid = (kc >= kc0) & (kc < kc0 + NA_WIN_COLS)
    off = GRID_W - NA_WIN_COLS
    padded = jnp.pad(rpb.astype(F32), ((0, 0), (0, 0), (off, off)))
    cols = jnp.stack([padded[:, :, GRID_W - 1 - q:2 * GRID_W - 1 - q] for q in range(GRID_W)], axis=2)
    cols = jnp.where(valid[None, None], cols, NEG_INF)
    tab = jnp.stack([cols[:, p:p + NA_WIN_ROWS] for p in range(NA_WIN_ROWS)], axis=0)
    tab = tab.transpose(0, 1, 3, 2, 4)
    return tab.reshape(NA_WIN_ROWS, NA_HEADS * GRID_W, NA_WIN_ROWS * GRID_W)


def _na_attn(z, bias_tab, bsz, seq):
    n = z.shape[0]
    width = NA_HEADS * HEAD_DIM
    rows = seq // GRID_W
    assert rows >= NA_WIN_ROWS
    return pl.pallas_call(
        functools.partial(_na_kernel, rows=rows),
        grid=(bsz,),
        in_specs=[pl.BlockSpec((seq, width), lambda b: (b, C_QN // width)),
                  pl.BlockSpec((seq, width), lambda b: (b, C_KN // width)),
                  pl.BlockSpec((seq, width), lambda b: (b, C_VN // width)),
                  pl.BlockSpec(bias_tab.shape, lambda b: (0, 0, 0))],
        out_specs=pl.BlockSpec((seq, width), lambda b: (b, 0)),
        out_shape=jax.ShapeDtypeStruct((n, width), BF16),
        compiler_params=_params("arbitrary"),
        name="na_attn",
    )(z, z, z, bias_tab)


def _dil_kernel(q_ref, k_ref, v_ref, o_ref, lse_ref, *scratch, dil, seg, win, tq, nqb):
    width = DIL_HEADS * HEAD_DIM
    nlc = width // LANES
    qi = pl.program_id(1)
    if dil > 1:
        kd_scr, vd_scr, stage_scr, ostage_scr, lstage_scr = scratch

        @pl.when(qi == 0)
        def _():
            for src, dst in ((k_ref, kd_scr), (v_ref, vd_scr)):
                for lc in range(nlc):
                    stage_scr[lc] = src[:, lc * LANES:(lc + 1) * LANES].astype(F32)
                for r in range(dil):
                    for lc in range(nlc):
                        dst[r * seg:(r + 1) * seg, lc * LANES:(lc + 1) * LANES] = (
                            stage_scr[lc, pl.ds(r, seg, stride=dil), :].astype(BF16))

        for lc in range(nlc):
            stage_scr[lc, 0:nqb * tq * dil, :] = q_ref[:, lc * LANES:(lc + 1) * LANES].astype(F32)
        k_src, v_src = kd_scr, vd_scr
    else:
        k_src, v_src = k_ref, v_ref

    def block(r, u):
        a0 = (qi * nqb + u) * tq
        w0 = jnp.clip(a0 - DIL_SIDE, 0, seg - win)
        aq = a0 + lax.broadcasted_iota(I32, (DIL_HEADS * tq, win), 0) % tq
        ak = w0 + lax.broadcasted_iota(I32, (DIL_HEADS * tq, win), 1)
        valid = jnp.abs(aq - ak) <= DIL_SIDE
        row0 = u * tq * dil
        if dil > 1:
            q = jnp.concatenate([stage_scr[lc, pl.ds(row0 + r, tq, stride=dil), :] for lc in range(nlc)],
                                axis=1).astype(BF16)
        else:
            q = q_ref[row0:row0 + tq, :]
        start = pl.multiple_of(r * seg + w0, DIL_SIDE)
        kw = k_src[pl.ds(start, win), :]
        vw = v_src[pl.ds(start, win), :]
        q4, masks = _stack_heads(q, DIL_HEADS)
        s = jnp.where(valid, _nt_dot(q4, kw), NEG_INF)
        mx = jnp.max(s, axis=-1, keepdims=True)
        p = jnp.exp(s - mx)
        l = jnp.sum(p, axis=-1, keepdims=True)
        o = _unstack_heads(jnp.dot(p.astype(BF16), vw, preferred_element_type=F32) / l, masks)
        lse = _unstack_heads(jnp.broadcast_to(mx + jnp.log(l), (DIL_HEADS * tq, width)), masks)
        if dil > 1:
            for lc in range(nlc):
                ostage_scr[lc, pl.ds(row0 + r, tq, stride=dil), :] = o[:, lc * LANES:(lc + 1) * LANES]
                lstage_scr[lc, pl.ds(row0 + r, tq, stride=dil), :] = lse[:, lc * LANES:(lc + 1) * LANES]
        else:
            o_ref[row0:row0 + tq, :] = o
            lse_ref[row0:row0 + tq, :] = lse

    def residue(r):
        for u in range(nqb):
            block(r, u)

    if dil == 1:
        residue(0)
    else:
        def body(r, carry):
            residue(r)
            return carry

        lax.fori_loop(0, dil, body, 0, unroll=2)
        for lc in range(nlc):
            o_ref[:, lc * LANES:(lc + 1) * LANES] = ostage_scr[lc]
            lse_ref[:, lc * LANES:(lc + 1) * LANES] = lstage_scr[lc]


def _dil_attn(z, g, bsz, seq):
    dil = DIL_DILATIONS[g]
    n = z.shape[0]
    width = DIL_HEADS * HEAD_DIM
    seg = seq // dil
    tq = min(128, seg)
    win = min(2 * DIL_SIDE + tq, seg)
    nqb = max(1, min(4, seg // tq, DIL_MAX_STEP_ROWS // (tq * dil)))
    nq = seg // (tq * nqb)
    rows = nqb * tq * dil
    cq, ck, cv = (C_QD // width + g, C_KD // width + g, C_VD // width + g)
    scratch = []
    if dil > 1:
        nlc = width // LANES
        scratch = [pltpu.VMEM((seq, width), BF16), pltpu.VMEM((seq, width), BF16),
                   pltpu.VMEM((nlc, seq, LANES), F32),
                   pltpu.VMEM((nlc, rows, LANES), F32), pltpu.VMEM((nlc, rows, LANES), F32)]
    return pl.pallas_call(
        functools.partial(_dil_kernel, dil=dil, seg=seg, win=win, tq=tq, nqb=nqb),
        grid=(bsz, nq),
        in_specs=[pl.BlockSpec((rows, width), lambda b, i: (b * nq + i, cq)),
                  pl.BlockSpec((seq, width), lambda b, i: (b, ck)),
                  pl.BlockSpec((seq, width), lambda b, i: (b, cv))],
        out_specs=[pl.BlockSpec((rows, width), lambda b, i: (b * nq + i, 0)),
                   pl.BlockSpec((rows, width), lambda b, i: (b * nq + i, 0))],
        out_shape=[jax.ShapeDtypeStruct((n, width), F32), jax.ShapeDtypeStruct((n, width), F32)],
        scratch_shapes=scratch,
        compiler_params=_params("arbitrary", "arbitrary"),
        name=f"dil_attn_g{g}",
    )(z, z, z)


def _mix_out_kernel(x_ref, oa_ref, on_ref, od0_ref, od1_ref, od2_ref, l0_ref, l1_ref, l2_ref,
                    ga_ref, gn_ref, gd_ref, wpa_ref, wpn_ref, wpd_ref, wout_ref,
                    gm_ref, fg_ref, fsc_ref, fsh_ref, wr_ref, br_ref,
                    xo_ref, h2_ref, lg_ref, *, sub):
    def sig(g):
        return 1.0 / (1.0 + jnp.exp(-g.astype(F32)))

    for r0 in range(0, x_ref.shape[0], sub):
        rows = slice(r0, r0 + sub)
        l0, l1, l2 = l0_ref[rows, :], l1_ref[rows, :], l2_ref[rows, :]
        mx = jnp.maximum(jnp.maximum(l0, l1), l2)
        e0, e1, e2 = jnp.exp(l0 - mx), jnp.exp(l1 - mx), jnp.exp(l2 - mx)
        den = e0 + e1 + e2
        od = (e0 / den) * od0_ref[rows, :] + (e1 / den) * od1_ref[rows, :] + (e2 / den) * od2_ref[rows, :]
        merged = (sig(ga_ref[rows, :]) * jnp.dot(oa_ref[rows, :], wpa_ref[...], preferred_element_type=F32)
                  + sig(gn_ref[rows, :]) * jnp.dot(on_ref[rows, :], wpn_ref[...], preferred_element_type=F32)
                  + sig(gd_ref[rows, :]) * jnp.dot(od.astype(BF16), wpd_ref[...], preferred_element_type=F32))
        y = jnp.dot(merged.astype(BF16), wout_ref[...], preferred_element_type=F32)
        xn = x_ref[rows, :] + gm_ref[...] * y
        xo_ref[rows, :] = xn
        ms = jnp.mean(xn * xn, axis=-1, keepdims=True)
        h2 = xn * lax.rsqrt(ms + NORM_EPS) * fg_ref[...]
        h2 = h2 * (1.0 + fsc_ref[...]) + fsh_ref[...]
        lg_ref[:, rows] = lax.dot_general(wr_ref[...], h2, (((1,), (1,)), ((), ())), preferred_element_type=F32,
                                          precision=HIGHEST) + br_ref[...]
        for c in range(h2.shape[1] // LANES):
            h2_ref[pl.ds(r0 * SUBLANES + c, sub, stride=SUBLANES), :] = h2[:, c * LANES:(c + 1) * LANES]


def _mix_out(x2, z, o_a, o_n, o_d, lse_d, wpa, wpn, wpd, wout, gate_m, fg, fsc, fsh, wr, br, seq, tm=512, sub=256):
    n, d = x2.shape
    assert d // LANES == SUBLANES and C_GATE % d == 0
    spb = seq // tm
    gcol = C_GATE // d
    wd = DIL_HEADS * HEAD_DIM
    row = lambda w: pl.BlockSpec((tm, w), lambda i: (i, 0))
    full = lambda a: pl.BlockSpec(a.shape, lambda i: (0,) * a.ndim)
    perb = pl.BlockSpec((None, 1, d), lambda i: (i // spb, 0, 0))
    return pl.pallas_call(
        functools.partial(_mix_out_kernel, sub=min(sub, tm)),
        grid=(n // tm,),
        in_specs=[row(d), row(o_a.shape[1]), row(o_n.shape[1]), row(wd), row(wd), row(wd), row(wd), row(wd), row(wd),
                  pl.BlockSpec((tm, d), lambda i: (i, gcol)),
                  pl.BlockSpec((tm, d), lambda i: (i, gcol + 1)),
                  pl.BlockSpec((tm, d), lambda i: (i, gcol + 2)),
                  full(wpa), full(wpn), full(wpd), full(wout),
                  perb, full(fg), perb, perb, full(wr), full(br)],
        out_specs=[row(d), pl.BlockSpec((tm * SUBLANES, LANES), lambda i: (i, 0)),
                   pl.BlockSpec((wr.shape[0], tm), lambda i: (0, i))],
        out_shape=[jax.ShapeDtypeStruct((n, d), F32),
                   jax.ShapeDtypeStruct((n * SUBLANES, LANES), F32),
                   jax.ShapeDtypeStruct((wr.shape[0], n), F32)],
        compiler_params=_params("arbitrary"),
        name="mix_out",
    )(x2, o_a, o_n, o_d[0], o_d[1], o_d[2], lse_d[0], lse_d[1], lse_d[2], z, z, z,
      wpa, wpn, wpd, wout, gate_m, fg, fsc, fsh, wr, br)


def _route_kernel(lg_ref, e_ref, w_ref, cnt_ref):
    i = pl.program_id(0)
    le = lg_ref[0:N_EXPERTS, :]
    gl = lg_ref[N_EXPERTS:N_EXPERTS + N_GROUPS, :]
    tb = le.shape[1]
    gmax = jnp.max(gl, axis=0, keepdims=True)
    grow = lax.broadcasted_iota(I32, gl.shape, 0)
    gidx = jnp.min(jnp.where(gl == gmax, grow, N_GROUPS), axis=0, keepdims=True)
    pg_top = 1.0 / jnp.sum(jnp.exp(gl - gmax), axis=0, keepdims=True)
    erow = lax.broadcasted_iota(I32, le.shape, 0)
    lm = jnp.where((erow // EXPERTS_PER_GROUP) == gidx, le, -jnp.inf)
    m1 = jnp.max(lm, axis=0, keepdims=True)
    i1 = jnp.min(jnp.where(lm == m1, erow, N_EXPERTS), axis=0, keepdims=True)
    lm2 = jnp.where(erow == i1, -jnp.inf, lm)
    m2 = jnp.max(lm2, axis=0, keepdims=True)
    i2 = jnp.min(jnp.where(lm2 == m2, erow, N_EXPERTS), axis=0, keepdims=True)
    t = jnp.exp(m2 - m1)
    e_ref[0:1, :] = i1
    e_ref[1:2, :] = i2
    w_ref[0:1, :] = pg_top / (1.0 + t)
    w_ref[1:2, :] = pg_top * t / (1.0 + t)
    oh = (erow == i1).astype(F32) + (erow == i2).astype(F32)
    cnt = jnp.sum(oh, axis=1, keepdims=True)

    @pl.when(i == 0)
    def _():
        cnt_ref[...] = jnp.zeros_like(cnt_ref)

    cnt_ref[...] += jnp.broadcast_to(cnt, cnt_ref.shape)


def _route(lgt, tb=1024):
    rows, n = lgt.shape
    tb = min(tb, n)
    return pl.pallas_call(
        _route_kernel,
        grid=(n // tb,),
        in_specs=[pl.BlockSpec((rows, tb), lambda i: (0, i))],
        out_specs=[pl.BlockSpec((2, tb), lambda i: (0, i)),
                   pl.BlockSpec((2, tb), lambda i: (0, i)),
                   pl.BlockSpec((N_EXPERTS, LANES), lambda i: (0, 0))],
        out_shape=[jax.ShapeDtypeStruct((2, n), I32),
                   jax.ShapeDtypeStruct((2, n), F32),
                   jax.ShapeDtypeStruct((N_EXPERTS, LANES), F32)],
        compiler_params=_params("arbitrary"),
        name="route",
    )(lgt)


def _rank_kernel(e_ref, cnt_ref, dest_ref, be_ref, nu_ref, zs_ref, carry_scr, base_scr, *, blk):
    i = pl.program_id(0)

    @pl.when(i == 0)
    def _():
        cnt = cnt_ref[...]
        padded = jnp.floor((cnt + (blk - 1)) / blk) * blk
        r = lax.broadcasted_iota(I32, cnt.shape, 0)
        c = lax.broadcasted_iota(I32, cnt.shape, 1)
        prow = jnp.sum(jnp.where(r == c, padded, 0.0), axis=0, keepdims=True)
        pad_end = jnp.sum(jnp.where(c <= r, prow, 0.0), axis=1, keepdims=True)
        base = jnp.broadcast_to(pad_end, cnt.shape) - padded
        base_scr[...] = base
        carry_scr[...] = jnp.zeros_like(carry_scr)
        zs_ref[...] = jnp.sum(jnp.where(r == c, base + cnt, 0.0) + jnp.where(r + N_EXPERTS == c, padded - cnt, 0.0),
                              axis=0, keepdims=True).astype(I32)
        jb = lax.broadcasted_iota(I32, (N_EXPERTS, be_ref.shape[1]), 1).astype(F32) * blk
        be = jnp.sum((pad_end <= jb).astype(I32), axis=0, keepdims=True)
        be_ref[...] = jnp.minimum(be, N_EXPERTS - 1)
        nu_ref[...] = jnp.broadcast_to(jnp.max(pad_end, axis=0, keepdims=True) / blk, nu_ref.shape).astype(I32)

    e = e_ref[...]
    tb = e.shape[1]
    erow = lax.broadcasted_iota(I32, (N_EXPERTS, tb), 0)
    oh0 = erow == e[0:1]
    oh1 = erow == e[1:2]
    both = jnp.where(oh0 | oh1, 1.0, 0.0)
    upper = jnp.where(lax.broadcasted_iota(I32, (tb, tb), 0) < lax.broadcasted_iota(I32, (tb, tb), 1), 1.0, 0.0)
    prefix = jnp.dot(both.astype(BF16), upper.astype(BF16), preferred_element_type=F32)
    tot = prefix + base_scr[:, 0:1] + carry_scr[:, 0:1]
    dest_ref[0:1, :] = jnp.sum(jnp.where(oh0, tot, 0.0), axis=0, keepdims=True).astype(I32)
    dest_ref[1:2, :] = jnp.sum(jnp.where(oh1, tot, 0.0), axis=0, keepdims=True).astype(I32)
    carry_scr[...] += jnp.broadcast_to(jnp.sum(both, axis=1, keepdims=True), carry_scr.shape)


def _rank(eidx, cnt, n_blocks, blk, tb=512):
    n = eidx.shape[1]
    tb = min(tb, n)
    nbp = -(-n_blocks // LANES) * LANES
    return pl.pallas_call(
        functools.partial(_rank_kernel, blk=blk),
        grid=(n // tb,),
        in_specs=[pl.BlockSpec((2, tb), lambda i: (0, i)),
                  pl.BlockSpec(cnt.shape, lambda i: (0, 0))],
        out_specs=[pl.BlockSpec((2, tb), lambda i: (0, i)),
                   pl.BlockSpec((1, nbp), lambda i: (0, 0)),
                   pl.BlockSpec((1, LANES), lambda i: (0, 0)),
                   pl.BlockSpec((1, LANES), lambda i: (0, 0))],
        out_shape=[jax.ShapeDtypeStruct((2, n), I32),
                   jax.ShapeDtypeStruct((1, nbp), I32),
                   jax.ShapeDtypeStruct((1, LANES), I32),
                   jax.ShapeDtypeStruct((1, LANES), I32)],
        scratch_shapes=[pltpu.VMEM((N_EXPERTS, LANES), F32), pltpu.VMEM((N_EXPERTS, LANES), F32)],
        compiler_params=_params("arbitrary"),
        name="rank",
    )(eidx, cnt)


def _tile_copy(src, src_row, dst, dst_row, sem):
    return pltpu.make_async_copy(src.at[pl.ds(pl.multiple_of(src_row * SUBLANES, SUBLANES), SUBLANES), :],
                                 dst.at[pl.ds(pl.multiple_of(dst_row * SUBLANES, SUBLANES), SUBLANES), :], sem)


def _invert_kernel(zs_ref, nu_ref, d0_ref, d1_ref, st_ref, *, tb, blk, total_blocks):
    i = pl.program_id(0)
    width = 8

    def zero_range(lo, hi):
        def trip(g, carry):
            for u in range(width):
                st_ref[jnp.minimum(lo + g * width + u, hi - 1)] = 0
            return carry

        lax.fori_loop(0, (hi - lo + width - 1) // width, trip, 0)

    @pl.when(i == 0)
    def _():
        for e in range(N_EXPERTS):
            zero_range(zs_ref[e], zs_ref[e] + zs_ref[N_EXPERTS + e])
        zero_range(nu_ref[0] * blk, total_blocks * blk)

    base = i * tb

    def body(t, carry):
        st_ref[d0_ref[t]] = base + t
        st_ref[d1_ref[t]] = base + t
        return carry

    lax.fori_loop(0, tb, body, 0, unroll=8)


def _invert(zstart, n_used, dest, n_blocks, blk, tb=512):
    n = dest.shape[1]
    tb = min(tb, n)
    total = n_blocks * blk
    grid_spec = pltpu.PrefetchScalarGridSpec(
        num_scalar_prefetch=2,
        grid=(n // tb,),
        in_specs=[pl.BlockSpec((tb,), lambda i, zs, nu: (i,), memory_space=pltpu.SMEM),
                  pl.BlockSpec((tb,), lambda i, zs, nu: (i,), memory_space=pltpu.SMEM)],
        out_specs=pl.BlockSpec((total,), lambda i, zs, nu: (0,), memory_space=pltpu.SMEM),
    )
    slot_tok = pl.pallas_call(
        functools.partial(_invert_kernel, tb=tb, blk=blk, total_blocks=n_blocks),
        grid_spec=grid_spec,
        out_shape=jax.ShapeDtypeStruct((total,), I32),
        compiler_params=_params("arbitrary", unchecked_indices=True),
        name="invert",
    )(zstart, n_used, dest[0], dest[1])
    return slot_tok.reshape(1, total)


def _expert_kernel(be_ref, first_ref, nu_ref, scur_ref, snext_ref, h_hbm, w1_ref, w3_ref, w2_ref, y_ref,
                   xg_scr, xb_scr, w1_scr, w3_scr, w2_scr, gsem, *, blk):
    del be_ref
    j = pl.program_id(0)
    nchunk = xb_scr.shape[1] // LANES

    def wait_rows():
        pltpu.make_async_copy(h_hbm.at[pl.ds(0, blk * SUBLANES), :], xg_scr, gsem).wait()

    def gather_next(lo, hi):
        for t in range(lo, hi):
            _tile_copy(h_hbm, snext_ref[0, t], xg_scr, t, gsem).start()

    @pl.when(j == 0)
    def _():
        def issue(t, carry):
            _tile_copy(h_hbm, scur_ref[0, t], xg_scr, t, gsem).start()
            return carry

        lax.fori_loop(0, blk, issue, 0)

    @pl.when(j < nu_ref[0])
    def _():
        @pl.when(first_ref[j] == 1)
        def _():
            w1_scr[...] = w1_ref[...].astype(BF16)
            w3_scr[...] = w3_ref[...].astype(BF16)
            w2_scr[...] = w2_ref[...].astype(BF16)

        wait_rows()
        for c in range(nchunk):
            xb_scr[:, c * LANES:(c + 1) * LANES] = xg_scr[pl.ds(c, blk, stride=SUBLANES), :].astype(BF16)
        xb = xb_scr[...]
        third = -(-blk // 3)
        gather_next(0, third)
        a = jnp.dot(xb, w1_scr[...], preferred_element_type=F32)
        gather_next(third, 2 * third)
        b = jnp.dot(xb, w3_scr[...], preferred_element_type=F32)
        gather_next(2 * third, blk)
        hid = (a / (1.0 + jnp.exp(-a))) * b
        y = jnp.dot(hid.astype(BF16), w2_scr[...], preferred_element_type=F32)
        for c in range(nchunk):
            y_ref[pl.ds(c, blk, stride=SUBLANES), :] = y[:, c * LANES:(c + 1) * LANES]

        @pl.when(j == nu_ref[0] - 1)
        def _():
            wait_rows()

    @pl.when(j >= nu_ref[0])
    def _():
        y_ref[...] = jnp.zeros_like(y_ref)


def _experts(block_e, first, n_used, slot_tok, h2t, w1, w3, w2, layer, n_blocks, blk):
    d, hid = w1.shape[2], w1.shape[3]
    grid_spec = pltpu.PrefetchScalarGridSpec(
        num_scalar_prefetch=3,
        grid=(n_blocks,),
        in_specs=[pl.BlockSpec((1, blk), lambda j, be, fi, nu: (0, 0), memory_space=pltpu.SMEM),
                  pl.BlockSpec((1, blk), lambda j, be, fi, nu: (0, jnp.minimum(j + 1, nu[0] - 1)),
                               memory_space=pltpu.SMEM),
                  pl.BlockSpec(memory_space=pl.ANY),
                  pl.BlockSpec((None, None, d, hid), lambda j, be, fi, nu: (layer, be[j], 0, 0)),
                  pl.BlockSpec((None, None, d, hid), lambda j, be, fi, nu: (layer, be[j], 0, 0)),
                  pl.BlockSpec((None, None, hid, d), lambda j, be, fi, nu: (layer, be[j], 0, 0))],
        out_specs=pl.BlockSpec((blk * SUBLANES, LANES), lambda j, be, fi, nu: (j, 0)),
        scratch_shapes=[pltpu.VMEM((blk * SUBLANES, LANES), F32), pltpu.VMEM((blk, d), BF16),
                        pltpu.VMEM((d, hid), BF16), pltpu.VMEM((d, hid), BF16), pltpu.VMEM((hid, d), BF16),
                        pltpu.SemaphoreType.DMA],
    )
    return pl.pallas_call(
        functools.partial(_expert_kernel, blk=blk),
        grid_spec=grid_spec,
        out_shape=jax.ShapeDtypeStruct((n_blocks * blk * SUBLANES, LANES), F32),
        compiler_params=_params("arbitrary"),
        name="experts",
    )(block_e, first, n_used, slot_tok, slot_tok, h2t, w1, w3, w2)


def _combine_kernel(dcur_ref, dnext_ref, x_ref, w_ref, gf_ref, fg_ref, y_hbm, xo_ref, g_scr, sems, *, tb, nsteps):
    i = pl.program_id(0)
    slot = i % 2

    @pl.when(i == 0)
    def _():
        def issue(t, carry):
            for k in range(2):
                _tile_copy(y_hbm, dcur_ref[k, t], g_scr.at[0, k], t, sems.at[0]).start()
            return carry

        lax.fori_loop(0, tb, issue, 0)

    @pl.when(i + 1 < nsteps)
    def _():
        for t in range(tb):
            for k in range(2):
                _tile_copy(y_hbm, dnext_ref[k, t], g_scr.at[1 - slot, k], t, sems.at[1 - slot]).start()

    for k in range(2):
        pltpu.make_async_copy(y_hbm.at[pl.ds(0, tb * SUBLANES), :], g_scr.at[slot, k], sems.at[slot]).wait()
    w0 = w_ref[:, 0:1]
    w1 = w_ref[:, 1:2]
    for c in range(x_ref.shape[1] // LANES):
        sl = slice(c * LANES, (c + 1) * LANES)
        yc = (w0 * g_scr[slot, 0, pl.ds(c, tb, stride=SUBLANES), :]
              + w1 * g_scr[slot, 1, pl.ds(c, tb, stride=SUBLANES), :])
        xo_ref[:, sl] = x_ref[:, sl] + gf_ref[:, sl] * yc
    xn = xo_ref[...]
    ms = jnp.mean(xn * xn, axis=-1, keepdims=True)
    xo_ref[...] = xn * lax.rsqrt(ms + NORM_EPS) * fg_ref[...]


def _combine_final(dest, x2, wts_t, gate_f, y, final_g, seq, tb=256):
    n, d = x2.shape
    spb = seq // tb
    nsteps = n // tb
    return pl.pallas_call(
        functools.partial(_combine_kernel, tb=tb, nsteps=nsteps),
        grid=(nsteps,),
        in_specs=[pl.BlockSpec((2, tb), lambda i: (0, i), memory_space=pltpu.SMEM),
                  pl.BlockSpec((2, tb), lambda i: (0, jnp.minimum(i + 1, nsteps - 1)), memory_space=pltpu.SMEM),
                  pl.BlockSpec((tb, d), lambda i: (i, 0)),
                  pl.BlockSpec((tb, 2), lambda i: (i, 0)),
                  pl.BlockSpec((None, 1, d), lambda i: (i // spb, 0, 0)),
                  pl.BlockSpec((1, d), lambda i: (0, 0)),
                  pl.BlockSpec(memory_space=pl.ANY)],
        out_specs=pl.BlockSpec((tb, d), lambda i: (i, 0)),
        out_shape=jax.ShapeDtypeStruct((n, d), F32),
        scratch_shapes=[pltpu.VMEM((2, 2, tb * SUBLANES, LANES), F32), pltpu.SemaphoreType.DMA((2,))],
        compiler_params=_params("arbitrary", unchecked_indices=True),
        name="combine_final",
    )(dest, dest, x2, wts_t, gate_f, final_g, y)


def _rotary_tables(seq, width):
    inv = 1.0 / (ROPE_THETA ** (jnp.arange(0, HEAD_DIM, 2, dtype=F32) / HEAD_DIM))
    ang = jnp.arange(seq, dtype=F32)[:, None] * inv[None, :]
    ang = jnp.concatenate([ang, ang], axis=-1)
    sign = jnp.concatenate([-jnp.ones((HEAD_DIM // 2,), F32), jnp.ones((HEAD_DIM // 2,), F32)])
    reps = width // HEAD_DIM
    return jnp.tile(jnp.cos(ang), (1, reps)), jnp.tile(jnp.sin(ang) * sign[None, :], (1, reps))


def _col_scale():
    colscale = jnp.ones((1, NC), F32)
    for lo, hi in ((C_QA, C_KA), (C_QD, C_KD), (C_QN, C_KN)):
        colscale = colscale.at[:, lo:hi].set(HEAD_DIM ** -0.5)
    return colscale.at[:, C_QA:C_KA].multiply(math.log2(math.e))


def kernel(x, c, ada_w, ada_b, mix_norm_g, ffn_norm_g, w_in, da_lambda, da_subln_g, na_rpb, w_proj_a, w_proj_n, w_proj_d, w_out, router_group_w, router_group_b, router_expert_w, router_expert_b, expert_w1, expert_w3, expert_w2, final_norm_g):
    bsz, seq, d = x.shape
    depth = ada_w.shape[0]
    n = bsz * seq
    tn = 512
    assert w_in.shape[2] == NC

    mods = _ada(c, ada_w, ada_b).reshape(depth, bsz, 6, 1, d)
    cos_t, sin_t = _rotary_tables(seq, LANES)
    colscale = _col_scale()

    n_blocks = (2 * n) // MOE_BLK + N_EXPERTS
    rpad = SUBLANES - N_GROUPS

    x2 = x.reshape(n, d)
    moe = None
    for l in range(depth):
        lam_init = 0.8 - 0.6 * math.exp(-0.3 * l)
        shift_m, scale_m, gate_m, shift_f, scale_f, gate_f = (mods[l, :, k] for k in range(6))
        w_bf = jnp.concatenate([w_in[l, :, lo:hi] for lo, hi in IN_PERM], axis=1).astype(BF16)
        proj_args = (mix_norm_g[l].reshape(1, d), scale_m, shift_m, cos_t, sin_t, colscale, w_bf, seq)
        if moe is None:
            z = _inproj(x2, *proj_args, tn=tn)
        else:
            x2, z = _combine_inproj(moe[0], x2, *moe[1:], *proj_args, tn=tn)
        o_a = _diff_attn(z, da_lambda[l], da_subln_g[l].reshape(1, 2 * HEAD_DIM), lam_init, bsz, seq)
        o_n = _na_attn(z, _na_bias_table(na_rpb[l]), bsz, seq)
        dil = [_dil_attn(z, g, bsz, seq) for g in range(len(DIL_DILATIONS))]
        wr = jnp.concatenate([router_expert_w[l], router_group_w[l], jnp.zeros((d, rpad), F32)], axis=1).T
        br = jnp.concatenate([router_expert_b[l], router_group_b[l], jnp.zeros((rpad,), F32)]).reshape(-1, 1)
        x2, h2t, lgt = _mix_out(
            x2, z, o_a, o_n, [o for o, _ in dil], [s for _, s in dil],
            w_proj_a[l].astype(BF16), w_proj_n[l].astype(BF16), w_proj_d[l].astype(BF16), w_out[l].astype(BF16),
            gate_m, ffn_norm_g[l].reshape(1, d), scale_f, shift_f, wr, br, seq)
        eidx, wts, cnt = _route(lgt)
        dest, block_e, n_used, zstart = _rank(eidx, cnt, n_blocks, MOE_BLK)
        block_e = block_e[0, :n_blocks]
        first = jnp.concatenate([jnp.ones((1,), I32), (block_e[1:] != block_e[:-1]).astype(I32)])
        n_used = n_used[0, :1]
        slot_tok = _invert(zstart[0, :2 * N_EXPERTS], n_used, dest, n_blocks, MOE_BLK)
        y = _experts(block_e, first, n_used, slot_tok, h2t, expert_w1, expert_w3, expert_w2, l, n_blocks, MOE_BLK)
        moe = (dest, wts.T, gate_f, y)
    return _combine_final(moe[0], x2, *moe[1:], final_norm_g.reshape(1, d), seq).reshape(bsz, seq, d)
```

```python
import functools
import math

import jax
import jax.numpy as jnp
from jax import lax
from jax.experimental import pallas as pl
from jax.experimental.pallas import tpu as pltpu

F32 = jnp.float32
BF16 = jnp.bfloat16
I32 = jnp.int32
HIGHEST = lax.Precision.HIGHEST

HEAD_DIM = 64
ROPE_THETA = 10000.0
NORM_EPS = 1e-6
NEG_INF = -1e30

DA_HEADS = 4
GRID_W = 64
NA_HEADS = 4
NA_WIN_ROWS = 8
NA_WIN_COLS = 16
DIL_DILATIONS = (1, 4, 16)
DIL_SIDE = 64
DIL_HEADS = 4
DIL_MAX_STEP_ROWS = 2048
N_GROUPS = 4
EXPERTS_PER_GROUP = 8
N_EXPERTS = N_GROUPS * EXPERTS_PER_GROUP

LANES = 128
SUBLANES = 8
MOE_BLK = 512
VMEM_LIMIT = 56 * 1024 * 1024

C_QA, C_KA, C_QD, C_KD = 0, 512, 1024, 1792
N_ROT = 2560
C_VA, C_GATE, C_QN, C_KN, C_VN, C_VD = 2560, 3072, 6144, 6400, 6656, 6912
NC = 7680
IN_PERM = ((0, 1024), (2304, 3840), (1024, 1536), (4608, 7680), (1536, 2304), (3840, 4608))


def _nt_dot(a, b):
    return lax.dot_general(a, b, (((1,), (1,)), ((), ())), preferred_element_type=F32)


def _stack_heads(q, nh):
    lane = lax.broadcasted_iota(I32, q.shape, 1)
    masks = [(lane >= HEAD_DIM * h) & (lane < HEAD_DIM * (h + 1)) for h in range(nh)]
    return jnp.concatenate([jnp.where(hm, q, jnp.zeros_like(q)) for hm in masks], axis=0), masks


def _unstack_heads(x, masks):
    m = x.shape[0] // len(masks)
    out = x[0:m]
    for h in range(1, len(masks)):
        out = jnp.where(masks[h], x[h * m:(h + 1) * m], out)
    return out


def _params(*sem):
    return pltpu.CompilerParams(dimension_semantics=sem, vmem_limit_bytes=VMEM_LIMIT)


def _ada_kernel(c_ref, w_ref, b_ref, o_ref):
    c = c_ref[...]
    ca = c / (1.0 + jnp.exp(-c))
    o_ref[...] = jnp.dot(ca, w_ref[...], preferred_element_type=F32, precision=HIGHEST) + b_ref[...]


def _ada(c, ada_w, ada_b):
    depth, d, w6 = ada_w.shape
    bsz = c.shape[0]
    rows = -(-bsz // SUBLANES) * SUBLANES
    cp = jnp.zeros((rows, d), F32).at[:bsz].set(c)
    tn = w6 // 4
    out = pl.pallas_call(
        _ada_kernel,
        grid=(depth, w6 // tn),
        in_specs=[pl.BlockSpec((rows, d), lambda l, j: (0, 0)),
                  pl.BlockSpec((None, d, tn), lambda l, j: (l, 0, j)),
                  pl.BlockSpec((None, 1, tn), lambda l, j: (l, 0, j))],
        out_specs=pl.BlockSpec((None, rows, tn), lambda l, j: (l, 0, j)),
        out_shape=jax.ShapeDtypeStruct((depth, rows, w6), F32),
        compiler_params=_params("arbitrary", "arbitrary"),
        name="ada_mod",
    )(cp, ada_w, ada_b.reshape(depth, 1, w6))
    return out[:, :bsz]


def _norm_modulate(x, g_ref, sc_ref, sh_ref):
    ms = jnp.mean(x * x, axis=-1, keepdims=True)
    y = x * lax.rsqrt(ms + NORM_EPS) * g_ref[...]
    return y * (1.0 + sc_ref[...]) + sh_ref[...]


def _project_chunk(j, tn, h_scr, w_scr, cs_ref, cos_ref, sin_ref, z_ref):
    cols = slice(j * tn, (j + 1) * tn)
    acc = jnp.dot(h_scr[...], w_scr[:, cols], preferred_element_type=F32) * cs_ref[:, cols]
    if j < N_ROT // tn:
        half = HEAD_DIM // 2
        lane = lax.broadcasted_iota(I32, (acc.shape[0], LANES), 1)
        lo = (lane % HEAD_DIM) < half
        cos = cos_ref[...]
        sin = sin_ref[...]
        for cc in range(tn // LANES):
            a = acc[:, cc * LANES:(cc + 1) * LANES]
            rot = jnp.where(lo, pltpu.roll(a, LANES - half, 1), pltpu.roll(a, half, 1))
            c0 = j * tn + cc * LANES
            z_ref[:, c0:c0 + LANES] = (a * cos + rot * sin).astype(BF16)
    else:
        z_ref[:, cols] = acc.astype(BF16)


def _inproj_kernel(x_ref, g_ref, sc_ref, sh_ref, cos_ref, sin_ref, cs_ref, w_hbm, z_ref, h_scr, w_scr, sem, *, tn):
    @pl.when(pl.program_id(0) == 0)
    def _():
        cp = pltpu.make_async_copy(w_hbm, w_scr, sem)
        cp.start()
        cp.wait()

    h_scr[...] = _norm_modulate(x_ref[...], g_ref, sc_ref, sh_ref).astype(BF16)
    for j in range(w_scr.shape[1] // tn):
        _project_chunk(j, tn, h_scr, w_scr, cs_ref, cos_ref, sin_ref, z_ref)


def _combine_inproj_kernel(dcur_ref, dnext_ref, x_ref, wt_ref, gf_ref, y_hbm,
                           g_ref, sc_ref, sh_ref, cos_ref, sin_ref, cs_ref, w_hbm,
                           xo_ref, z_ref, h_scr, w_scr, g_scr, wsem, gsem, *, tn, tb, nsteps):
    i = pl.program_id(0)

    @pl.when(i == 0)
    def _():
        cp = pltpu.make_async_copy(w_hbm, w_scr, wsem)
        cp.start()

        def issue(t, carry):
            for k in range(2):
                _tile_copy(y_hbm, dcur_ref[k, t], g_scr.at[k], t, gsem).start()
            return carry

        lax.fori_loop(0, tb, issue, 0)
        cp.wait()

    for k in range(2):
        pltpu.make_async_copy(y_hbm.at[pl.ds(0, tb * SUBLANES), :], g_scr.at[k], gsem).wait()
    w0 = wt_ref[:, 0:1]
    w1 = wt_ref[:, 1:2]
    for c in range(x_ref.shape[1] // LANES):
        sl = slice(c * LANES, (c + 1) * LANES)
        yc = w0 * g_scr[0, pl.ds(c, tb, stride=SUBLANES), :] + w1 * g_scr[1, pl.ds(c, tb, stride=SUBLANES), :]
        xo_ref[:, sl] = x_ref[:, sl] + gf_ref[:, sl] * yc
    h_scr[...] = _norm_modulate(xo_ref[...], g_ref, sc_ref, sh_ref).astype(BF16)

    nchunks = w_scr.shape[1] // tn
    per = -(-tb // nchunks)
    for j in range(nchunks):
        _project_chunk(j, tn, h_scr, w_scr, cs_ref, cos_ref, sin_ref, z_ref)
        for t in range(j * per, min((j + 1) * per, tb)):
            for k in range(2):
                _tile_copy(y_hbm, dnext_ref[k, t], g_scr.at[k], t, gsem).start()

    @pl.when(i == nsteps - 1)
    def _():
        for k in range(2):
            pltpu.make_async_copy(y_hbm.at[pl.ds(0, tb * SUBLANES), :], g_scr.at[k], gsem).wait()


def _combine_inproj(dest, x2, wts_t, gate_f, y, g, scale, shift, cos_t, sin_t, colscale, w_bf, seq, tm=512, tn=512):
    n, d = x2.shape
    nc = w_bf.shape[1]
    assert N_ROT % tn == 0 and nc % tn == 0
    spb = seq // tm
    nsteps = n // tm
    perb = pl.BlockSpec((None, 1, d), lambda i: (i // spb, 0, 0))
    return pl.pallas_call(
        functools.partial(_combine_inproj_kernel, tn=tn, tb=tm, nsteps=nsteps),
        grid=(nsteps,),
        in_specs=[pl.BlockSpec((2, tm), lambda i: (0, i), memory_space=pltpu.SMEM),
                  pl.BlockSpec((2, tm), lambda i: (0, jnp.minimum(i + 1, nsteps - 1)), memory_space=pltpu.SMEM),
                  pl.BlockSpec((tm, d), lambda i: (i, 0)),
                  pl.BlockSpec((tm, 2), lambda i: (i, 0)),
                  perb,
                  pl.BlockSpec(memory_space=pl.ANY),
                  pl.BlockSpec((1, d), lambda i: (0, 0)),
                  perb, perb,
                  pl.BlockSpec((tm, LANES), lambda i: (i % spb, 0)),
                  pl.BlockSpec((tm, LANES), lambda i: (i % spb, 0)),
                  pl.BlockSpec((1, nc), lambda i: (0, 0)),
                  pl.BlockSpec(memory_space=pl.ANY)],
        out_specs=[pl.BlockSpec((tm, d), lambda i: (i, 0)), pl.BlockSpec((tm, nc), lambda i: (i, 0))],
        out_shape=[jax.ShapeDtypeStruct((n, d), F32), jax.ShapeDtypeStruct((n, nc), BF16)],
        scratch_shapes=[pltpu.VMEM((tm, d), BF16), pltpu.VMEM((d, nc), BF16),
                        pltpu.VMEM((2, tm * SUBLANES, LANES), F32),
                        pltpu.SemaphoreType.DMA, pltpu.SemaphoreType.DMA],
        compiler_params=_params("arbitrary"),
        name="combine_inproj",
    )(dest, dest, x2, wts_t, gate_f, y, g, scale, shift, cos_t, sin_t, colscale, w_bf)


def _inproj(x2, g, scale, shift, cos_t, sin_t, colscale, w_bf, seq, tm=512, tn=512):
    n, d = x2.shape
    nc = w_bf.shape[1]
    assert N_ROT % tn == 0 and nc % tn == 0
    spb = seq // tm
    return pl.pallas_call(
        functools.partial(_inproj_kernel, tn=tn),
        grid=(n // tm,),
        in_specs=[pl.BlockSpec((tm, d), lambda i: (i, 0)),
                  pl.BlockSpec((1, d), lambda i: (0, 0)),
                  pl.BlockSpec((None, 1, d), lambda i: (i // spb, 0, 0)),
                  pl.BlockSpec((None, 1, d), lambda i: (i // spb, 0, 0)),
                  pl.BlockSpec((tm, LANES), lambda i: (i % spb, 0)),
                  pl.BlockSpec((tm, LANES), lambda i: (i % spb, 0)),
                  pl.BlockSpec((1, nc), lambda i: (0, 0)),
                  pl.BlockSpec(memory_space=pl.ANY)],
        out_specs=pl.BlockSpec((tm, nc), lambda i: (i, 0)),
        out_shape=jax.ShapeDtypeStruct((n, nc), BF16),
        scratch_shapes=[pltpu.VMEM((tm, d), BF16), pltpu.VMEM((d, nc), BF16), pltpu.SemaphoreType.DMA],
        compiler_params=_params("arbitrary"),
        name="inproj",
    )(x2, g, scale, shift, cos_t, sin_t, colscale, w_bf)


def _diff_attn_kernel(lam_ref, g_ref, q_ref, k_ref, v_ref, o_ref, *, lam_init, chunk):
    lp = lam_ref[...]
    lam = (jnp.exp(jnp.sum(lp[0:1] * lp[1:2], axis=-1, keepdims=True))
           - jnp.exp(jnp.sum(lp[2:3] * lp[3:4], axis=-1, keepdims=True)) + lam_init)
    k = k_ref[...]
    v = v_ref[...]
    lane = lax.broadcasted_iota(I32, (chunk, 2 * HEAD_DIM), 1)
    chains = [(c, m) for c in range(q_ref.shape[0] // chunk) for m in range(2)]

    def scores(c, m):
        q = q_ref[c * chunk:(c + 1) * chunk, :]
        qm = jnp.where((lane >= HEAD_DIM * m) & (lane < HEAD_DIM * (m + 1)), q, jnp.zeros_like(q))
        return _nt_dot(qm, k)

    ahead = 1
    pending = {i: scores(*chains[i]) for i in range(min(ahead, len(chains)))}
    pb, ls = {}, {}
    for i, (c, m) in enumerate(chains):
        if i + ahead < len(chains):
            pending[i + ahead] = scores(*chains[i + ahead])
        s = pending.pop(i)
        mx = jnp.max(s, axis=-1, keepdims=True)
        p = jnp.exp2(s - mx)
        ls[m] = jnp.sum(p, axis=-1, keepdims=True)
        pb[m] = p.astype(BF16)
        if m == 1:
            ratio = (lam * ls[0] / ls[1]).astype(BF16)
            a = pb[0] - pb[1] * ratio
            o = jnp.dot(a, v, preferred_element_type=F32) / ls[0]
            ms = jnp.mean(o * o, axis=-1, keepdims=True)
            o = o * lax.rsqrt(ms + NORM_EPS) * g_ref[...] * (1.0 - lam_init)
            o_ref[c * chunk:(c + 1) * chunk, :] = o.astype(BF16)


def _diff_attn(z, lam_p, subln_g, lam_init, bsz, seq, tq=2048, chunk=256):
    n = z.shape[0]
    hw = 2 * HEAD_DIM
    tq = min(tq, seq)
    nq = seq // tq
    return pl.pallas_call(
        functools.partial(_diff_attn_kernel, lam_init=lam_init, chunk=min(chunk, tq)),
        grid=(bsz, DA_HEADS, nq),
        in_specs=[pl.BlockSpec((4, HEAD_DIM), lambda b, h, i: (0, 0)),
                  pl.BlockSpec((1, hw), lambda b, h, i: (0, 0)),
                  pl.BlockSpec((tq, hw), lambda b, h, i: (b * nq + i, C_QA // hw + h)),
                  pl.BlockSpec((seq, hw), lambda b, h, i: (b, C_KA // hw + h)),
                  pl.BlockSpec((seq, hw), lambda b, h, i: (b, C_VA // hw + h))],
        out_specs=pl.BlockSpec((tq, hw), lambda b, h, i: (b * nq + i, h)),
        out_shape=jax.ShapeDtypeStruct((n, DA_HEADS * hw), BF16),
        compiler_params=_params("arbitrary", "arbitrary", "arbitrary"),
        name="diff_attn",
    )(lam_p, subln_g, z, z, z)


def _na_kernel(q_ref, k_ref, v_ref, bias_ref, o_ref, *, rows):
    win = NA_WIN_ROWS * GRID_W

    def body(r, carry):
        kr0 = jnp.clip(r - NA_WIN_ROWS // 2, 0, rows - NA_WIN_ROWS)
        pat = kr0 - r + (NA_WIN_ROWS - 1)
        q4, masks = _stack_heads(q_ref[pl.ds(pl.multiple_of(r * GRID_W, GRID_W), GRID_W), :], NA_HEADS)
        kw = k_ref[pl.ds(pl.multiple_of(kr0 * GRID_W, GRID_W), win), :]
        vw = v_ref[pl.ds(pl.multiple_of(kr0 * GRID_W, GRID_W), win), :]
        s = _nt_dot(q4, kw) + bias_ref[pat]
        mx = jnp.max(s, axis=-1, keepdims=True)
        p = jnp.exp(s - mx)
        l = jnp.sum(p, axis=-1, keepdims=True)
        o4 = jnp.dot(p.astype(BF16), vw, preferred_element_type=F32) / l
        o_ref[pl.ds(pl.multiple_of(r * GRID_W, GRID_W), GRID_W), :] = _unstack_heads(o4, masks).astype(BF16)
        return carry

    lax.fori_loop(0, rows, body, 0, unroll=8)


def _na_bias_table(rpb):
    c = jnp.arange(GRID_W)[:, None]
    kc = jnp.arange(GRID_W)[None, :]
    kc0 = jnp.clip(c - NA_WIN_COLS // 2, 0, GRID_W - NA_WIN_COLS)
    valid = (kc >= kc0) & (kc < kc0 + NA_WIN_COLS)
    off = GRID_W - NA_WIN_COLS
    padded = jnp.pad(rpb.astype(F32), ((0, 0), (0, 0), (off, off)))
    cols = jnp.stack([padded[:, :, GRID_W - 1 - q:2 * GRID_W - 1 - q] for q in range(GRID_W)], axis=2)
    cols = jnp.where(valid[None, None], cols, NEG_INF)
    tab = jnp.stack([cols[:, p:p + NA_WIN_ROWS] for p in range(NA_WIN_ROWS)], axis=0)
    tab = tab.transpose(0, 1, 3, 2, 4)
    return tab.reshape(NA_WIN_ROWS, NA_HEADS * GRID_W, NA_WIN_ROWS * GRID_W)


def _na_attn(z, bias_tab, bsz, seq):
    n = z.shape[0]
    width = NA_HEADS * HEAD_DIM
    rows = seq // GRID_W
    assert rows >= NA_WIN_ROWS
    return pl.pallas_call(
        functools.partial(_na_kernel, rows=rows),
        grid=(bsz,),
        in_specs=[pl.BlockSpec((seq, width), lambda b: (b, C_QN // width)),
                  pl.BlockSpec((seq, width), lambda b: (b, C_KN // width)),
                  pl.BlockSpec((seq, width), lambda b: (b, C_VN // width)),
                  pl.BlockSpec(bias_tab.shape, lambda b: (0, 0, 0))],
        out_specs=pl.BlockSpec((seq, width), lambda b: (b, 0)),
        out_shape=jax.ShapeDtypeStruct((n, width), BF16),
        compiler_params=_params("arbitrary"),
        name="na_attn",
    )(z, z, z, bias_tab)


def _dil_kernel(q_ref, k_ref, v_ref, o_ref, lse_ref, *scratch, dil, seg, win, tq, nqb):
    width = DIL_HEADS * HEAD_DIM
    nlc = width // LANES
    qi = pl.program_id(1)
    if dil > 1:
        kd_scr, vd_scr, stage_scr, ostage_scr, lstage_scr = scratch

        @pl.when(qi == 0)
        def _():
            for src, dst in ((k_ref, kd_scr), (v_ref, vd_scr)):
                for lc in range(nlc):
                    stage_scr[lc] = src[:, lc * LANES:(lc + 1) * LANES].astype(F32)
                for r in range(dil):
                    for lc in range(nlc):
                        dst[r * seg:(r + 1) * seg, lc * LANES:(lc + 1) * LANES] = (
                            stage_scr[lc, pl.ds(r, seg, stride=dil), :].astype(BF16))

        for lc in range(nlc):
            stage_scr[lc, 0:nqb * tq * dil, :] = q_ref[:, lc * LANES:(lc + 1) * LANES].astype(F32)
        k_src, v_src = kd_scr, vd_scr
    else:
        k_src, v_src = k_ref, v_ref

    def block(r, u):
        a0 = (qi * nqb + u) * tq
        w0 = jnp.clip(a0 - DIL_SIDE, 0, seg - win)
        aq = a0 + lax.broadcasted_iota(I32, (DIL_HEADS * tq, win), 0) % tq
        ak = w0 + lax.broadcasted_iota(I32, (DIL_HEADS * tq, win), 1)
        valid = jnp.abs(aq - ak) <= DIL_SIDE
        row0 = u * tq * dil
        if dil > 1:
            q = jnp.concatenate([stage_scr[lc, pl.ds(row0 + r, tq, stride=dil), :] for lc in range(nlc)],
                                axis=1).astype(BF16)
        else:
            q = q_ref[row0:row0 + tq, :]
        start = pl.multiple_of(r * seg + w0, DIL_SIDE)
        kw = k_src[pl.ds(start, win), :]
        vw = v_src[pl.ds(start, win), :]
        q4, masks = _stack_heads(q, DIL_HEADS)
        s = jnp.where(valid, _nt_dot(q4, kw), NEG_INF)
        mx = jnp.max(s, axis=-1, keepdims=True)
        p = jnp.exp(s - mx)
        l = jnp.sum(p, axis=-1, keepdims=True)
        o = _unstack_heads(jnp.dot(p.astype(BF16), vw, preferred_element_type=F32) / l, masks)
        lse = _unstack_heads(jnp.broadcast_to(mx + jnp.log(l), (DIL_HEADS * tq, width)), masks)
        if dil > 1:
            for lc in range(nlc):
                ostage_scr[lc, pl.ds(row0 + r, tq, stride=dil), :] = o[:, lc * LANES:(lc + 1) * LANES]
                lstage_scr[lc, pl.ds(row0 + r, tq, stride=dil), :] = lse[:, lc * LANES:(lc + 1) * LANES]
        else:
            o_ref[row0:row0 + tq, :] = o
            lse_ref[row0:row0 + tq, :] = lse

    def residue(r):
        for u in range(nqb):
            block(r, u)

    if dil == 1:
        residue(0)
    else:
        def body(r, carry):
            residue(r)
            return carry

        lax.fori_loop(0, dil, body, 0, unroll=max(2, 4 // nqb))
        for lc in range(nlc):
            o_ref[:, lc * LANES:(lc + 1) * LANES] = ostage_scr[lc]
            lse_ref[:, lc * LANES:(lc + 1) * LANES] = lstage_scr[lc]


def _dil_attn(z, g, bsz, seq):
    dil = DIL_DILATIONS[g]
    n = z.shape[0]
    width = DIL_HEADS * HEAD_DIM
    seg = seq // dil
    tq = min(128, seg)
    win = min(2 * DIL_SIDE + tq, seg)
    nqb = max(1, min(4, seg // tq, DIL_MAX_STEP_ROWS // (tq * dil)))
    nq = seg // (tq * nqb)
    rows = nqb * tq * dil
    cq, ck, cv = (C_QD // width + g, C_KD // width + g, C_VD // width + g)
    scratch = []
    if dil > 1:
        nlc = width // LANES
        scratch = [pltpu.VMEM((seq, width), BF16), pltpu.VMEM((seq, width), BF16),
                   pltpu.VMEM((nlc, seq, LANES), F32),
                   pltpu.VMEM((nlc, rows, LANES), F32), pltpu.VMEM((nlc, rows, LANES), F32)]
    return pl.pallas_call(
        functools.partial(_dil_kernel, dil=dil, seg=seg, win=win, tq=tq, nqb=nqb),
        grid=(bsz, nq),
        in_specs=[pl.BlockSpec((rows, width), lambda b, i: (b * nq + i, cq)),
                  pl.BlockSpec((seq, width), lambda b, i: (b, ck)),
                  pl.BlockSpec((seq, width), lambda b, i: (b, cv))],
        out_specs=[pl.BlockSpec((rows, width), lambda b, i: (b * nq + i, 0)),
                   pl.BlockSpec((rows, width), lambda b, i: (b * nq + i, 0))],
        out_shape=[jax.ShapeDtypeStruct((n, width), F32), jax.ShapeDtypeStruct((n, width), F32)],
        scratch_shapes=scratch,
        compiler_params=_params("arbitrary", "arbitrary"),
        name=f"dil_attn_g{g}",
    )(z, z, z)


def _mix_out_kernel(x_ref, oa_ref, on_ref, od0_ref, od1_ref, od2_ref, l0_ref, l1_ref, l2_ref,
                    ga_ref, gn_ref, gd_ref, wpa_ref, wpn_ref, wpd_ref, wout_ref,
                    gm_ref, fg_ref, fsc_ref, fsh_ref, wr_ref, br_ref,
                    xo_ref, h2_ref, lg_ref, *, sub):
    def sig(g):
        return 1.0 / (1.0 + jnp.exp(-g.astype(F32)))

    for r0 in range(0, x_ref.shape[0], sub):
        rows = slice(r0, r0 + sub)
        l0, l1, l2 = l0_ref[rows, :], l1_ref[rows, :], l2_ref[rows, :]
        mx = jnp.maximum(jnp.maximum(l0, l1), l2)
        e0, e1, e2 = jnp.exp(l0 - mx), jnp.exp(l1 - mx), jnp.exp(l2 - mx)
        den = e0 + e1 + e2
        od = (e0 / den) * od0_ref[rows, :] + (e1 / den) * od1_ref[rows, :] + (e2 / den) * od2_ref[rows, :]
        merged = (sig(ga_ref[rows, :]) * jnp.dot(oa_ref[rows, :], wpa_ref[...], preferred_element_type=F32)
                  + sig(gn_ref[rows, :]) * jnp.dot(on_ref[rows, :], wpn_ref[...], preferred_element_type=F32)
                  + sig(gd_ref[rows, :]) * jnp.dot(od.astype(BF16), wpd_ref[...], preferred_element_type=F32))
        y = jnp.dot(merged.astype(BF16), wout_ref[...], preferred_element_type=F32)
        xn = x_ref[rows, :] + gm_ref[...] * y
        xo_ref[rows, :] = xn
        ms = jnp.mean(xn * xn, axis=-1, keepdims=True)
        h2 = xn * lax.rsqrt(ms + NORM_EPS) * fg_ref[...]
        h2 = h2 * (1.0 + fsc_ref[...]) + fsh_ref[...]
        lg_ref[:, rows] = lax.dot_general(wr_ref[...], h2, (((1,), (1,)), ((), ())), preferred_element_type=F32,
                                          precision=HIGHEST) + br_ref[...]
        for c in range(h2.shape[1] // LANES):
            h2_ref[pl.ds(r0 * SUBLANES + c, sub, stride=SUBLANES), :] = h2[:, c * LANES:(c + 1) * LANES]


def _mix_out(x2, z, o_a, o_n, o_d, lse_d, wpa, wpn, wpd, wout, gate_m, fg, fsc, fsh, wr, br, seq, tm=512, sub=256):
    n, d = x2.shape
    assert d // LANES == SUBLANES and C_GATE % d == 0
    spb = seq // tm
    gcol = C_GATE // d
    wd = DIL_HEADS * HEAD_DIM
    row = lambda w: pl.BlockSpec((tm, w), lambda i: (i, 0))
    full = lambda a: pl.BlockSpec(a.shape, lambda i: (0,) * a.ndim)
    perb = pl.BlockSpec((None, 1, d), lambda i: (i // spb, 0, 0))
    return pl.pallas_call(
        functools.partial(_mix_out_kernel, sub=min(sub, tm)),
        grid=(n // tm,),
        in_specs=[row(d), row(o_a.shape[1]), row(o_n.shape[1]), row(wd), row(wd), row(wd), row(wd), row(wd), row(wd),
                  pl.BlockSpec((tm, d), lambda i: (i, gcol)),
                  pl.BlockSpec((tm, d), lambda i: (i, gcol + 1)),
                  pl.BlockSpec((tm, d), lambda i: (i, gcol + 2)),
                  full(wpa), full(wpn), full(wpd), full(wout),
                  perb, full(fg), perb, perb, full(wr), full(br)],
        out_specs=[row(d), pl.BlockSpec((tm * SUBLANES, LANES), lambda i: (i, 0)),
                   pl.BlockSpec((wr.shape[0], tm), lambda i: (0, i))],
        out_shape=[jax.ShapeDtypeStruct((n, d), F32),
                   jax.ShapeDtypeStruct((n * SUBLANES, LANES), F32),
                   jax.ShapeDtypeStruct((wr.shape[0], n), F32)],
        compiler_params=_params("arbitrary"),
        name="mix_out",
    )(x2, o_a, o_n, o_d[0], o_d[1], o_d[2], lse_d[0], lse_d[1], lse_d[2], z, z, z,
      wpa, wpn, wpd, wout, gate_m, fg, fsc, fsh, wr, br)


def _route_kernel(lg_ref, e_ref, w_ref, cnt_ref):
    i = pl.program_id(0)
    le = lg_ref[0:N_EXPERTS, :]
    gl = lg_ref[N_EXPERTS:N_EXPERTS + N_GROUPS, :]
    tb = le.shape[1]
    gmax = jnp.max(gl, axis=0, keepdims=True)
    grow = lax.broadcasted_iota(I32, gl.shape, 0)
    gidx = jnp.min(jnp.where(gl == gmax, grow, N_GROUPS), axis=0, keepdims=True)
    pg_top = 1.0 / jnp.sum(jnp.exp(gl - gmax), axis=0, keepdims=True)
    erow = lax.broadcasted_iota(I32, le.shape, 0)
    lm = jnp.where((erow // EXPERTS_PER_GROUP) == gidx, le, -jnp.inf)
    m1 = jnp.max(lm, axis=0, keepdims=True)
    i1 = jnp.min(jnp.where(lm == m1, erow, N_EXPERTS), axis=0, keepdims=True)
    lm2 = jnp.where(erow == i1, -jnp.inf, lm)
    m2 = jnp.max(lm2, axis=0, keepdims=True)
    i2 = jnp.min(jnp.where(lm2 == m2, erow, N_EXPERTS), axis=0, keepdims=True)
    t = jnp.exp(m2 - m1)
    e_ref[0:1, :] = i1
    e_ref[1:2, :] = i2
    w_ref[0:1, :] = pg_top / (1.0 + t)
    w_ref[1:2, :] = pg_top * t / (1.0 + t)
    oh = (erow == i1).astype(F32) + (erow == i2).astype(F32)
    cnt = jnp.sum(oh, axis=1, keepdims=True)

    @pl.when(i == 0)
    def _():
        cnt_ref[...] = jnp.zeros_like(cnt_ref)

    cnt_ref[...] += jnp.broadcast_to(cnt, cnt_ref.shape)


def _route(lgt, tb=1024):
    rows, n = lgt.shape
    tb = min(tb, n)
    return pl.pallas_call(
        _route_kernel,
        grid=(n // tb,),
        in_specs=[pl.BlockSpec((rows, tb), lambda i: (0, i))],
        out_specs=[pl.BlockSpec((2, tb), lambda i: (0, i)),
                   pl.BlockSpec((2, tb), lambda i: (0, i)),
                   pl.BlockSpec((N_EXPERTS, LANES), lambda i: (0, 0))],
        out_shape=[jax.ShapeDtypeStruct((2, n), I32),
                   jax.ShapeDtypeStruct((2, n), F32),
                   jax.ShapeDtypeStruct((N_EXPERTS, LANES), F32)],
        compiler_params=_params("arbitrary"),
        name="route",
    )(lgt)


def _rank_kernel(e_ref, cnt_ref, dest_ref, be_ref, nu_ref, zs_ref, carry_scr, base_scr, *, blk):
    i = pl.program_id(0)

    @pl.when(i == 0)
    def _():
        cnt = cnt_ref[...]
        padded = jnp.floor((cnt + (blk - 1)) / blk) * blk
        r = lax.broadcasted_iota(I32, cnt.shape, 0)
        c = lax.broadcasted_iota(I32, cnt.shape, 1)
        prow = jnp.sum(jnp.where(r == c, padded, 0.0), axis=0, keepdims=True)
        pad_end = jnp.sum(jnp.where(c <= r, prow, 0.0), axis=1, keepdims=True)
        base = jnp.broadcast_to(pad_end, cnt.shape) - padded
        base_scr[...] = base
        carry_scr[...] = jnp.zeros_like(carry_scr)
        zs_ref[...] = jnp.sum(jnp.where(r == c, base + cnt, 0.0) + jnp.where(r + N_EXPERTS == c, padded - cnt, 0.0),
                              axis=0, keepdims=True).astype(I32)
        jb = lax.broadcasted_iota(I32, (N_EXPERTS, be_ref.shape[1]), 1).astype(F32) * blk
        be = jnp.sum((pad_end <= jb).astype(I32), axis=0, keepdims=True)
        be_ref[...] = jnp.minimum(be, N_EXPERTS - 1)
        nu_ref[...] = jnp.broadcast_to(jnp.max(pad_end, axis=0, keepdims=True) / blk, nu_ref.shape).astype(I32)

    e = e_ref[...]
    tb = e.shape[1]
    erow = lax.broadcasted_iota(I32, (N_EXPERTS, tb), 0)
    oh0 = erow == e[0:1]
    oh1 = erow == e[1:2]
    both = jnp.where(oh0 | oh1, 1.0, 0.0)
    upper = jnp.where(lax.broadcasted_iota(I32, (tb, tb), 0) < lax.broadcasted_iota(I32, (tb, tb), 1), 1.0, 0.0)
    prefix = jnp.dot(both.astype(BF16), upper.astype(BF16), preferred_element_type=F32)
    tot = prefix + base_scr[:, 0:1] + carry_scr[:, 0:1]
    dest_ref[0:1, :] = jnp.sum(jnp.where(oh0, tot, 0.0), axis=0, keepdims=True).astype(I32)
    dest_ref[1:2, :] = jnp.sum(jnp.where(oh1, tot, 0.0), axis=0, keepdims=True).astype(I32)
    carry_scr[...] += jnp.broadcast_to(jnp.sum(both, axis=1, keepdims=True), carry_scr.shape)


def _rank(eidx, cnt, n_blocks, blk, tb=512):
    n = eidx.shape[1]
    tb = min(tb, n)
    nbp = -(-n_blocks // LANES) * LANES
    return pl.pallas_call(
        functools.partial(_rank_kernel, blk=blk),
        grid=(n // tb,),
        in_specs=[pl.BlockSpec((2, tb), lambda i: (0, i)),
                  pl.BlockSpec(cnt.shape, lambda i: (0, 0))],
        out_specs=[pl.BlockSpec((2, tb), lambda i: (0, i)),
                   pl.BlockSpec((1, nbp), lambda i: (0, 0)),
                   pl.BlockSpec((1, LANES), lambda i: (0, 0)),
                   pl.BlockSpec((1, LANES), lambda i: (0, 0))],
        out_shape=[jax.ShapeDtypeStruct((2, n), I32),
                   jax.ShapeDtypeStruct((1, nbp), I32),
                   jax.ShapeDtypeStruct((1, LANES), I32),
                   jax.ShapeDtypeStruct((1, LANES), I32)],
        scratch_shapes=[pltpu.VMEM((N_EXPERTS, LANES), F32), pltpu.VMEM((N_EXPERTS, LANES), F32)],
        compiler_params=_params("arbitrary"),
        name="rank",
    )(eidx, cnt)


def _tile_copy(src, src_row, dst, dst_row, sem):
    return pltpu.make_async_copy(src.at[pl.ds(pl.multiple_of(src_row * SUBLANES, SUBLANES), SUBLANES), :],
                                 dst.at[pl.ds(pl.multiple_of(dst_row * SUBLANES, SUBLANES), SUBLANES), :], sem)


def _dispatch_kernel(zs_ref, nu_ref, dest_ref, h_ref, xs_hbm, zero_scr, sem, zsem, *, tb, blk, total_blocks):
    def zero_fill(slot, nslots):
        start = pl.multiple_of(slot * SUBLANES, SUBLANES)
        return pltpu.make_async_copy(zero_scr.at[pl.ds(0, nslots * SUBLANES), :],
                                     xs_hbm.at[pl.ds(start, nslots * SUBLANES), :], zsem)

    def pad_fills(act):
        for e in range(N_EXPERTS):
            off = zs_ref[e]
            count = zs_ref[N_EXPERTS + e]
            for b in range(blk.bit_length() - 1):
                bit = (count >> b) & 1

                @pl.when(bit == 1)
                def _(off=off, b=b):
                    act(zero_fill(off, 1 << b))

                off = off + bit * (1 << b)

    @pl.when(pl.program_id(0) == 0)
    def _():
        zero_scr[...] = jnp.zeros_like(zero_scr)
        pad_fills(lambda cp: cp.start())

        def tail_start(j, carry):
            zero_fill(j * blk, blk).start()
            return carry

        def tail_wait(j, carry):
            zero_fill(j * blk, blk).wait()
            return carry

        lax.fori_loop(nu_ref[0], total_blocks, tail_start, 0)
        pad_fills(lambda cp: cp.wait())
        lax.fori_loop(nu_ref[0], total_blocks, tail_wait, 0)

    def issue(t, carry):
        for k in range(2):
            _tile_copy(h_ref, t, xs_hbm, dest_ref[k, t], sem).start()
        return carry

    lax.fori_loop(0, tb, issue, 0)
    for k in range(2):
        pltpu.make_async_copy(h_ref, xs_hbm.at[pl.ds(0, tb * SUBLANES), :], sem).wait()


def _dispatch(zstart, n_used, dest, h2t, n_blocks, blk, tb=512):
    n = dest.shape[1]
    tb = min(tb, n)
    grid_spec = pltpu.PrefetchScalarGridSpec(
        num_scalar_prefetch=2,
        grid=(n // tb,),
        in_specs=[pl.BlockSpec((2, tb), lambda i, zs, nu: (0, i), memory_space=pltpu.SMEM),
                  pl.BlockSpec((tb * SUBLANES, LANES), lambda i, zs, nu: (i, 0))],
        out_specs=pl.BlockSpec(memory_space=pl.ANY),
        scratch_shapes=[pltpu.VMEM((blk * SUBLANES, LANES), F32), pltpu.SemaphoreType.DMA, pltpu.SemaphoreType.DMA],
    )
    total_blocks = n_blocks
    return pl.pallas_call(
        functools.partial(_dispatch_kernel, tb=tb, blk=blk, total_blocks=total_blocks),
        grid_spec=grid_spec,
        out_shape=jax.ShapeDtypeStruct((total_blocks * blk * SUBLANES, LANES), F32),
        compiler_params=pltpu.CompilerParams(dimension_semantics=("arbitrary",), has_side_effects=True),
        name="dispatch",
    )(zstart, n_used, dest, h2t)


def _expert_kernel(be_ref, first_ref, nu_ref, xs_ref, w1_ref, w3_ref, w2_ref, y_ref, xb_scr, w1_scr, w3_scr, w2_scr, *, blk):
    del be_ref
    j = pl.program_id(0)
    nchunk = xb_scr.shape[1] // LANES

    @pl.when(j < nu_ref[0])
    def _():
        @pl.when(first_ref[j] == 1)
        def _():
            w1_scr[...] = w1_ref[...].astype(BF16)
            w3_scr[...] = w3_ref[...].astype(BF16)
            w2_scr[...] = w2_ref[...].astype(BF16)

        for c in range(nchunk):
            xb_scr[:, c * LANES:(c + 1) * LANES] = xs_ref[pl.ds(c, blk, stride=SUBLANES), :].astype(BF16)
        xb = xb_scr[...]
        a = jnp.dot(xb, w1_scr[...], preferred_element_type=F32)
        b = jnp.dot(xb, w3_scr[...], preferred_element_type=F32)
        hid = (a / (1.0 + jnp.exp(-a))) * b
        y = jnp.dot(hid.astype(BF16), w2_scr[...], preferred_element_type=F32)
        for c in range(nchunk):
            y_ref[pl.ds(c, blk, stride=SUBLANES), :] = y[:, c * LANES:(c + 1) * LANES]

    @pl.when(j >= nu_ref[0])
    def _():
        y_ref[...] = jnp.zeros_like(y_ref)


def _experts(block_e, first, n_used, xs, w1, w3, w2, layer, n_blocks, blk):
    d, hid = w1.shape[2], w1.shape[3]
    used = lambda j, nu: jnp.minimum(j, nu[0] - 1)
    grid_spec = pltpu.PrefetchScalarGridSpec(
        num_scalar_prefetch=3,
        grid=(n_blocks,),
        in_specs=[pl.BlockSpec((blk * SUBLANES, LANES), lambda j, be, fi, nu: (used(j, nu), 0)),
                  pl.BlockSpec((None, None, d, hid), lambda j, be, fi, nu: (layer, be[j], 0, 0)),
                  pl.BlockSpec((None, None, d, hid), lambda j, be, fi, nu: (layer, be[j], 0, 0)),
                  pl.BlockSpec((None, None, hid, d), lambda j, be, fi, nu: (layer, be[j], 0, 0))],
        out_specs=pl.BlockSpec((blk * SUBLANES, LANES), lambda j, be, fi, nu: (j, 0)),
        scratch_shapes=[pltpu.VMEM((blk, d), BF16), pltpu.VMEM((d, hid), BF16), pltpu.VMEM((d, hid), BF16),
                        pltpu.VMEM((hid, d), BF16)],
    )
    return pl.pallas_call(
        functools.partial(_expert_kernel, blk=blk),
        grid_spec=grid_spec,
        out_shape=jax.ShapeDtypeStruct((n_blocks * blk * SUBLANES, LANES), F32),
        compiler_params=_params("arbitrary"),
        name="experts",
    )(block_e, first, n_used, xs, w1, w3, w2)


def _combine_kernel(dcur_ref, dnext_ref, x_ref, w_ref, gf_ref, fg_ref, y_hbm, xo_ref, g_scr, sems, *, tb, nsteps):
    i = pl.program_id(0)
    slot = i % 2

    @pl.when(i == 0)
    def _():
        def issue(t, carry):
            for k in range(2):
                _tile_copy(y_hbm, dcur_ref[k, t], g_scr.at[0, k], t, sems.at[0]).start()
            return carry

        lax.fori_loop(0, tb, issue, 0)

    @pl.when(i + 1 < nsteps)
    def _():
        for t in range(tb):
            for k in range(2):
                _tile_copy(y_hbm, dnext_ref[k, t], g_scr.at[1 - slot, k], t, sems.at[1 - slot]).start()

    for k in range(2):
        pltpu.make_async_copy(y_hbm.at[pl.ds(0, tb * SUBLANES), :], g_scr.at[slot, k], sems.at[slot]).wait()
    w0 = w_ref[:, 0:1]
    w1 = w_ref[:, 1:2]
    for c in range(x_ref.shape[1] // LANES):
        sl = slice(c * LANES, (c + 1) * LANES)
        yc = (w0 * g_scr[slot, 0, pl.ds(c, tb, stride=SUBLANES), :]
              + w1 * g_scr[slot, 1, pl.ds(c, tb, stride=SUBLANES), :])
        xo_ref[:, sl] = x_ref[:, sl] + gf_ref[:, sl] * yc
    xn = xo_ref[...]
    ms = jnp.mean(xn * xn, axis=-1, keepdims=True)
    xo_ref[...] = xn * lax.rsqrt(ms + NORM_EPS) * fg_ref[...]


def _combine_final(dest, x2, wts_t, gate_f, y, final_g, seq, tb=256):
    n, d = x2.shape
    spb = seq // tb
    nsteps = n // tb
    return pl.pallas_call(
        functools.partial(_combine_kernel, tb=tb, nsteps=nsteps),
        grid=(nsteps,),
        in_specs=[pl.BlockSpec((2, tb), lambda i: (0, i), memory_space=pltpu.SMEM),
                  pl.BlockSpec((2, tb), lambda i: (0, jnp.minimum(i + 1, nsteps - 1)), memory_space=pltpu.SMEM),
                  pl.BlockSpec((tb, d), lambda i: (i, 0)),
                  pl.BlockSpec((tb, 2), lambda i: (i, 0)),
                  pl.BlockSpec((None, 1, d), lambda i: (i // spb, 0, 0)),
                  pl.BlockSpec((1, d), lambda i: (0, 0)),
                  pl.BlockSpec(memory_space=pl.ANY)],
        out_specs=pl.BlockSpec((tb, d), lambda i: (i, 0)),
        out_shape=jax.ShapeDtypeStruct((n, d), F32),
        scratch_shapes=[pltpu.VMEM((2, 2, tb * SUBLANES, LANES), F32), pltpu.SemaphoreType.DMA((2,))],
        compiler_params=_params("arbitrary"),
        name="combine_final",
    )(dest, dest, x2, wts_t, gate_f, final_g, y)


def _rotary_tables(seq, width):
    inv = 1.0 / (ROPE_THETA ** (jnp.arange(0, HEAD_DIM, 2, dtype=F32) / HEAD_DIM))
    ang = jnp.arange(seq, dtype=F32)[:, None] * inv[None, :]
    ang = jnp.concatenate([ang, ang], axis=-1)
    sign = jnp.concatenate([-jnp.ones((HEAD_DIM // 2,), F32), jnp.ones((HEAD_DIM // 2,), F32)])
    reps = width // HEAD_DIM
    return jnp.tile(jnp.cos(ang), (1, reps)), jnp.tile(jnp.sin(ang) * sign[None, :], (1, reps))


def _col_scale():
    colscale = jnp.ones((1, NC), F32)
    for lo, hi in ((C_QA, C_KA), (C_QD, C_KD), (C_QN, C_KN)):
        colscale = colscale.at[:, lo:hi].set(HEAD_DIM ** -0.5)
    return colscale.at[:, C_QA:C_KA].multiply(math.log2(math.e))


def kernel(x, c, ada_w, ada_b, mix_norm_g, ffn_norm_g, w_in, da_lambda, da_subln_g, na_rpb, w_proj_a, w_proj_n, w_proj_d, w_out, router_group_w, router_group_b, router_expert_w, router_expert_b, expert_w1, expert_w3, expert_w2, final_norm_g):
    bsz, seq, d = x.shape
    depth = ada_w.shape[0]
    n = bsz * seq
    tn = 512
    assert w_in.shape[2] == NC

    mods = _ada(c, ada_w, ada_b).reshape(depth, bsz, 6, 1, d)
    cos_t, sin_t = _rotary_tables(seq, LANES)
    colscale = _col_scale()

    n_blocks = (2 * n) // MOE_BLK + N_EXPERTS
    rpad = SUBLANES - N_GROUPS

    x2 = x.reshape(n, d)
    moe = None
    for l in range(depth):
        lam_init = 0.8 - 0.6 * math.exp(-0.3 * l)
        shift_m, scale_m, gate_m, shift_f, scale_f, gate_f = (mods[l, :, k] for k in range(6))
        w_bf = jnp.concatenate([w_in[l, :, lo:hi] for lo, hi in IN_PERM], axis=1).astype(BF16)
        proj_args = (mix_norm_g[l].reshape(1, d), scale_m, shift_m, cos_t, sin_t, colscale, w_bf, seq)
        if moe is None:
            z = _inproj(x2, *proj_args, tn=tn)
        else:
            x2, z = _combine_inproj(moe[0], x2, *moe[1:], *proj_args, tn=tn)
        o_a = _diff_attn(z, da_lambda[l], da_subln_g[l].reshape(1, 2 * HEAD_DIM), lam_init, bsz, seq)
        o_n = _na_attn(z, _na_bias_table(na_rpb[l]), bsz, seq)
        dil = [_dil_attn(z, g, bsz, seq) for g in range(len(DIL_DILATIONS))]
        wr = jnp.concatenate([router_expert_w[l], router_group_w[l], jnp.zeros((d, rpad), F32)], axis=1).T
        br = jnp.concatenate([router_expert_b[l], router_group_b[l], jnp.zeros((rpad,), F32)]).reshape(-1, 1)
        x2, h2t, lgt = _mix_out(
            x2, z, o_a, o_n, [o for o, _ in dil], [s for _, s in dil],
            w_proj_a[l].astype(BF16), w_proj_n[l].astype(BF16), w_proj_d[l].astype(BF16), w_out[l].astype(BF16),
            gate_m, ffn_norm_g[l].reshape(1, d), scale_f, shift_f, wr, br, seq)
        eidx, wts, cnt = _route(lgt)
        dest, block_e, n_used, zstart = _rank(eidx, cnt, n_blocks, MOE_BLK)
        block_e = block_e[0, :n_blocks]
        first = jnp.concatenate([jnp.ones((1,), I32), (block_e[1:] != block_e[:-1]).astype(I32)])
        n_used = n_used[0, :1]
        xs = _dispatch(zstart[0, :2 * N_EXPERTS], n_used, dest, h2t, n_blocks, MOE_BLK)
        y = _experts(block_e, first, n_used, xs, expert_w1, expert_w3, expert_w2, l, n_blocks, MOE_BLK)
        moe = (dest, wts.T, gate_f, y)
    return _combine_final(moe[0], x2, *moe[1:], final_norm_g.reshape(1, d), seq).reshape(bsz, seq, d)
```

```python
import functools
import math

import jax
import jax.numpy as jnp
from jax import lax
from jax.experimental import pallas as pl
from jax.experimental.pallas import tpu as pltpu

F32 = jnp.float32
BF16 = jnp.bfloat16
I32 = jnp.int32
HIGHEST = lax.Precision.HIGHEST

HEAD_DIM = 64
ROPE_THETA = 10000.0
NORM_EPS = 1e-6
NEG_INF = -1e30

DA_HEADS = 4
GRID_W = 64
NA_HEADS = 4
NA_WIN_ROWS = 8
NA_WIN_COLS = 16
DIL_DILATIONS = (1, 4, 16)
DIL_SIDE = 64
DIL_HEADS = 4
DIL_MAX_STEP_ROWS = 2048
N_GROUPS = 4
EXPERTS_PER_GROUP = 8
N_EXPERTS = N_GROUPS * EXPERTS_PER_GROUP

LANES = 128
SUBLANES = 8
MOE_BLK = 512
VMEM_LIMIT = 56 * 1024 * 1024

C_QA, C_KA, C_QD, C_KD = 0, 512, 1024, 1792
N_ROT = 2560
C_VA, C_GATE, C_QN, C_KN, C_VN, C_VD = 2560, 3072, 6144, 6400, 6656, 6912
NC = 7680
IN_PERM = ((0, 1024), (2304, 3840), (1024, 1536), (4608, 7680), (1536, 2304), (3840, 4608))


def _nt_dot(a, b):
    return lax.dot_general(a, b, (((1,), (1,)), ((), ())), preferred_element_type=F32)


def _sigmoid(x):
    return 0.5 * jnp.tanh(0.5 * x) + 0.5


def _stack_heads(q, nh):
    lane = lax.broadcasted_iota(I32, q.shape, 1)
    masks = [(lane >= HEAD_DIM * h) & (lane < HEAD_DIM * (h + 1)) for h in range(nh)]
    return jnp.concatenate([jnp.where(hm, q, jnp.zeros_like(q)) for hm in masks], axis=0), masks


def _unstack_heads(x, masks):
    m = x.shape[0] // len(masks)
    out = x[0:m]
    for h in range(1, len(masks)):
        out = jnp.where(masks[h], x[h * m:(h + 1) * m], out)
    return out


def _params(*sem):
    return pltpu.CompilerParams(dimension_semantics=sem, vmem_limit_bytes=VMEM_LIMIT)


def _ada_kernel(c_ref, w_ref, b_ref, o_ref):
    c = c_ref[...]
    ca = c / (1.0 + jnp.exp(-c))
    o_ref[...] = jnp.dot(ca, w_ref[...], preferred_element_type=F32, precision=HIGHEST) + b_ref[...]


def _ada(c, ada_w, ada_b):
    depth, d, w6 = ada_w.shape
    bsz = c.shape[0]
    rows = -(-bsz // SUBLANES) * SUBLANES
    cp = jnp.zeros((rows, d), F32).at[:bsz].set(c)
    tn = w6 // 4
    out = pl.pallas_call(
        _ada_kernel,
        grid=(depth, w6 // tn),
        in_specs=[pl.BlockSpec((rows, d), lambda l, j: (0, 0)),
                  pl.BlockSpec((None, d, tn), lambda l, j: (l, 0, j)),
                  pl.BlockSpec((None, 1, tn), lambda l, j: (l, 0, j))],
        out_specs=pl.BlockSpec((None, rows, tn), lambda l, j: (l, 0, j)),
        out_shape=jax.ShapeDtypeStruct((depth, rows, w6), F32),
        compiler_params=_params("arbitrary", "arbitrary"),
        name="ada_mod",
    )(cp, ada_w, ada_b.reshape(depth, 1, w6))
    return out[:, :bsz]


def _norm_modulate(x, g_ref, sc_ref, sh_ref):
    ms = jnp.mean(x * x, axis=-1, keepdims=True)
    y = x * lax.rsqrt(ms + NORM_EPS) * g_ref[...]
    return y * (1.0 + sc_ref[...]) + sh_ref[...]


def _project_chunk(j, tn, h_scr, w_scr, cs_ref, cos_ref, sin_ref, z_ref):
    cols = slice(j * tn, (j + 1) * tn)
    acc = jnp.dot(h_scr[...], w_scr[:, cols], preferred_element_type=F32) * cs_ref[:, cols]
    if j < N_ROT // tn:
        half = HEAD_DIM // 2
        lane = lax.broadcasted_iota(I32, (acc.shape[0], LANES), 1)
        lo = (lane % HEAD_DIM) < half
        cos = cos_ref[...]
        sin = sin_ref[...]
        for cc in range(tn // LANES):
            a = acc[:, cc * LANES:(cc + 1) * LANES]
            rot = jnp.where(lo, pltpu.roll(a, LANES - half, 1), pltpu.roll(a, half, 1))
            c0 = j * tn + cc * LANES
            z_ref[:, c0:c0 + LANES] = (a * cos + rot * sin).astype(BF16)
    else:
        z_ref[:, cols] = acc.astype(BF16)


def _inproj_kernel(x_ref, g_ref, sc_ref, sh_ref, cos_ref, sin_ref, cs_ref, w_hbm, z_ref, h_scr, w_scr, sem, *, tn, layer):
    @pl.when(pl.program_id(0) == 0)
    def _():
        cp = pltpu.make_async_copy(w_hbm.at[layer], w_scr, sem)
        cp.start()
        cp.wait()

    h_scr[...] = _norm_modulate(x_ref[...], g_ref, sc_ref, sh_ref).astype(BF16)
    for j in range(w_scr.shape[1] // tn):
        _project_chunk(j, tn, h_scr, w_scr, cs_ref, cos_ref, sin_ref, z_ref)


def _combine_inproj_kernel(dcur_ref, dnext_ref, x_ref, wt_ref, gf_ref, y_hbm,
                           g_ref, sc_ref, sh_ref, cos_ref, sin_ref, cs_ref, w_hbm,
                           xo_ref, z_ref, h_scr, w_scr, g_scr, wsem, gsem, *, tn, tb, nsteps, layer):
    i = pl.program_id(0)

    @pl.when(i == 0)
    def _():
        cp = pltpu.make_async_copy(w_hbm.at[layer], w_scr, wsem)
        cp.start()

        def issue(t, carry):
            for k in range(2):
                _tile_copy(y_hbm, dcur_ref[k, t], g_scr.at[k], t, gsem).start()
            return carry

        lax.fori_loop(0, tb, issue, 0)
        cp.wait()

    for k in range(2):
        pltpu.make_async_copy(y_hbm.at[pl.ds(0, tb * SUBLANES), :], g_scr.at[k], gsem).wait()
    w0 = wt_ref[:, 0:1]
    w1 = wt_ref[:, 1:2]
    for c in range(x_ref.shape[1] // LANES):
        sl = slice(c * LANES, (c + 1) * LANES)
        yc = w0 * g_scr[0, pl.ds(c, tb, stride=SUBLANES), :] + w1 * g_scr[1, pl.ds(c, tb, stride=SUBLANES), :]
        xo_ref[:, sl] = x_ref[:, sl] + gf_ref[:, sl] * yc
    h_scr[...] = _norm_modulate(xo_ref[...], g_ref, sc_ref, sh_ref).astype(BF16)

    nchunks = w_scr.shape[1] // tn
    per = -(-tb // nchunks)
    for j in range(nchunks):
        _project_chunk(j, tn, h_scr, w_scr, cs_ref, cos_ref, sin_ref, z_ref)
        for t in range(j * per, min((j + 1) * per, tb)):
            for k in range(2):
                _tile_copy(y_hbm, dnext_ref[k, t], g_scr.at[k], t, gsem).start()

    @pl.when(i == nsteps - 1)
    def _():
        for k in range(2):
            pltpu.make_async_copy(y_hbm.at[pl.ds(0, tb * SUBLANES), :], g_scr.at[k], gsem).wait()


def _combine_inproj(dest, x2, wts_t, gate_f, y, g, scale, shift, cos_t, sin_t, colscale, w_bf, layer, seq,
                    tm=512, tn=512):
    n, d = x2.shape
    nc = w_bf.shape[2]
    assert N_ROT % tn == 0 and nc % tn == 0
    spb = seq // tm
    nsteps = n // tm
    perb = pl.BlockSpec((None, 1, d), lambda i: (i // spb, 0, 0))
    return pl.pallas_call(
        functools.partial(_combine_inproj_kernel, tn=tn, tb=tm, nsteps=nsteps, layer=layer),
        grid=(nsteps,),
        in_specs=[pl.BlockSpec((2, tm), lambda i: (0, i), memory_space=pltpu.SMEM),
                  pl.BlockSpec((2, tm), lambda i: (0, jnp.minimum(i + 1, nsteps - 1)), memory_space=pltpu.SMEM),
                  pl.BlockSpec((tm, d), lambda i: (i, 0)),
                  pl.BlockSpec((tm, 2), lambda i: (i, 0)),
                  perb,
                  pl.BlockSpec(memory_space=pl.ANY),
                  pl.BlockSpec((1, d), lambda i: (0, 0)),
                  perb, perb,
                  pl.BlockSpec((tm, LANES), lambda i: (i % spb, 0)),
                  pl.BlockSpec((tm, LANES), lambda i: (i % spb, 0)),
                  pl.BlockSpec((1, nc), lambda i: (0, 0)),
                  pl.BlockSpec(memory_space=pl.ANY)],
        out_specs=[pl.BlockSpec((tm, d), lambda i: (i, 0)), pl.BlockSpec((tm, nc), lambda i: (i, 0))],
        out_shape=[jax.ShapeDtypeStruct((n, d), F32), jax.ShapeDtypeStruct((n, nc), BF16)],
        scratch_shapes=[pltpu.VMEM((tm, d), BF16), pltpu.VMEM((d, nc), BF16),
                        pltpu.VMEM((2, tm * SUBLANES, LANES), F32),
                        pltpu.SemaphoreType.DMA, pltpu.SemaphoreType.DMA],
        compiler_params=_params("arbitrary"),
        name="combine_inproj",
    )(dest, dest, x2, wts_t, gate_f, y, g, scale, shift, cos_t, sin_t, colscale, w_bf)


def _inproj(x2, g, scale, shift, cos_t, sin_t, colscale, w_bf, layer, seq, tm=512, tn=512):
    n, d = x2.shape
    nc = w_bf.shape[2]
    assert N_ROT % tn == 0 and nc % tn == 0
    spb = seq // tm
    return pl.pallas_call(
        functools.partial(_inproj_kernel, tn=tn, layer=layer),
        grid=(n // tm,),
        in_specs=[pl.BlockSpec((tm, d), lambda i: (i, 0)),
                  pl.BlockSpec((1, d), lambda i: (0, 0)),
                  pl.BlockSpec((None, 1, d), lambda i: (i // spb, 0, 0)),
                  pl.BlockSpec((None, 1, d), lambda i: (i // spb, 0, 0)),
                  pl.BlockSpec((tm, LANES), lambda i: (i % spb, 0)),
                  pl.BlockSpec((tm, LANES), lambda i: (i % spb, 0)),
                  pl.BlockSpec((1, nc), lambda i: (0, 0)),
                  pl.BlockSpec(memory_space=pl.ANY)],
        out_specs=pl.BlockSpec((tm, nc), lambda i: (i, 0)),
        out_shape=jax.ShapeDtypeStruct((n, nc), BF16),
        scratch_shapes=[pltpu.VMEM((tm, d), BF16), pltpu.VMEM((d, nc), BF16), pltpu.SemaphoreType.DMA],
        compiler_params=_params("arbitrary"),
        name="inproj",
    )(x2, g, scale, shift, cos_t, sin_t, colscale, w_bf)


def _diff_attn_kernel(lam_ref, g_ref, q_ref, k_ref, v_ref, o_ref, *, lam_init, chunk):
    lp = lam_ref[...]
    lam = (jnp.exp(jnp.sum(lp[0:1] * lp[1:2], axis=-1, keepdims=True))
           - jnp.exp(jnp.sum(lp[2:3] * lp[3:4], axis=-1, keepdims=True)) + lam_init)
    k = k_ref[...]
    v = v_ref[...]
    lane = lax.broadcasted_iota(I32, (chunk, 2 * HEAD_DIM), 1)
    chains = [(c, m) for c in range(q_ref.shape[0] // chunk) for m in range(2)]

    def scores(c, m):
        q = q_ref[c * chunk:(c + 1) * chunk, :]
        qm = jnp.where((lane >= HEAD_DIM * m) & (lane < HEAD_DIM * (m + 1)), q, jnp.zeros_like(q))
        return _nt_dot(qm, k)

    ahead = 1
    pending = {i: scores(*chains[i]) for i in range(min(ahead, len(chains)))}
    pb, ls = {}, {}
    for i, (c, m) in enumerate(chains):
        if i + ahead < len(chains):
            pending[i + ahead] = scores(*chains[i + ahead])
        s = pending.pop(i)
        mx = jnp.max(s, axis=-1, keepdims=True)
        p = jnp.exp2(s - mx)
        ls[m] = jnp.sum(p, axis=-1, keepdims=True)
        pb[m] = p.astype(BF16)
        if m == 1:
            ratio = (lam * ls[0] / ls[1]).astype(BF16)
            a = pb[0] - pb[1] * ratio
            o = jnp.dot(a, v, preferred_element_type=F32) / ls[0]
            ms = jnp.mean(o * o, axis=-1, keepdims=True)
            o = o * lax.rsqrt(ms + NORM_EPS) * g_ref[...] * (1.0 - lam_init)
            o_ref[c * chunk:(c + 1) * chunk, :] = o.astype(BF16)


def _diff_attn(z, lam_p, subln_g, lam_init, bsz, seq, tq=2048, chunk=256):
    n = z.shape[0]
    hw = 2 * HEAD_DIM
    tq = min(tq, seq)
    nq = seq // tq
    return pl.pallas_call(
        functools.partial(_diff_attn_kernel, lam_init=lam_init, chunk=min(chunk, tq)),
        grid=(bsz, DA_HEADS, nq),
        in_specs=[pl.BlockSpec((4, HEAD_DIM), lambda b, h, i: (0, 0)),
                  pl.BlockSpec((1, hw), lambda b, h, i: (0, 0)),
                  pl.BlockSpec((tq, hw), lambda b, h, i: (b * nq + i, C_QA // hw + h)),
                  pl.BlockSpec((seq, hw), lambda b, h, i: (b, C_KA // hw + h)),
                  pl.BlockSpec((seq, hw), lambda b, h, i: (b, C_VA // hw + h))],
        out_specs=pl.BlockSpec((tq, hw), lambda b, h, i: (b * nq + i, h)),
        out_shape=jax.ShapeDtypeStruct((n, DA_HEADS * hw), BF16),
        compiler_params=_params("arbitrary", "arbitrary", "arbitrary"),
        name="diff_attn",
    )(lam_p, subln_g, z, z, z)


def _na_kernel(q_ref, k_ref, v_ref, bias_ref, o_ref, *, rows):
    win = NA_WIN_ROWS * GRID_W

    def body(r, carry):
        kr0 = jnp.clip(r - NA_WIN_ROWS // 2, 0, rows - NA_WIN_ROWS)
        pat = kr0 - r + (NA_WIN_ROWS - 1)
        q4, masks = _stack_heads(q_ref[pl.ds(pl.multiple_of(r * GRID_W, GRID_W), GRID_W), :], NA_HEADS)
        kw = k_ref[pl.ds(pl.multiple_of(kr0 * GRID_W, GRID_W), win), :]
        vw = v_ref[pl.ds(pl.multiple_of(kr0 * GRID_W, GRID_W), win), :]
        s = _nt_dot(q4, kw) + bias_ref[pat]
        mx = jnp.max(s, axis=-1, keepdims=True)
        p = jnp.exp(s - mx)
        l = jnp.sum(p, axis=-1, keepdims=True)
        o4 = jnp.dot(p.astype(BF16), vw, preferred_element_type=F32) / l
        o_ref[pl.ds(pl.multiple_of(r * GRID_W, GRID_W), GRID_W), :] = _unstack_heads(o4, masks).astype(BF16)
        return carry

    lax.fori_loop(0, rows, body, 0, unroll=8)


def _na_bias_table(rpb):
    c = jnp.arange(GRID_W)[:, None]
    kc = jnp.arange(GRID_W)[None, :]
    kc0 = jnp.clip(c - NA_WIN_COLS // 2, 0, GRID_W - NA_WIN_COLS)
    valid = (kc >= kc0) & (kc < kc0 + NA_WIN_COLS)
    off = GRID_W - NA_WIN_COLS
    padded = jnp.pad(rpb.astype(F32), ((0, 0), (0, 0), (off, off)))
    cols = jnp.stack([padded[:, :, GRID_W - 1 - q:2 * GRID_W - 1 - q] for q in range(GRID_W)], axis=2)
    cols = jnp.where(valid[None, None], cols, NEG_INF)
    tab = jnp.stack([cols[:, p:p + NA_WIN_ROWS] for p in range(NA_WIN_ROWS)], axis=0)
    tab = tab.transpose(0, 1, 3, 2, 4)
    return tab.reshape(NA_WIN_ROWS, NA_HEADS * GRID_W, NA_WIN_ROWS * GRID_W)


def _na_attn(z, bias_tab, bsz, seq):
    n = z.shape[0]
    width = NA_HEADS * HEAD_DIM
    rows = seq // GRID_W
    assert rows >= NA_WIN_ROWS
    return pl.pallas_call(
        functools.partial(_na_kernel, rows=rows),
        grid=(bsz,),
        in_specs=[pl.BlockSpec((seq, width), lambda b: (b, C_QN // width)),
                  pl.BlockSpec((seq, width), lambda b: (b, C_KN // width)),
                  pl.BlockSpec((seq, width), lambda b: (b, C_VN // width)),
                  pl.BlockSpec(bias_tab.shape, lambda b: (0, 0, 0))],
        out_specs=pl.BlockSpec((seq, width), lambda b: (b, 0)),
        out_shape=jax.ShapeDtypeStruct((n, width), BF16),
        compiler_params=_params("arbitrary"),
        name="na_attn",
    )(z, z, z, bias_tab)


def _dil_kernel(q_ref, k_ref, v_ref, o_ref, lse_ref, *scratch, dil, seg, win, tq, nqb):
    width = DIL_HEADS * HEAD_DIM
    nlc = width // LANES
    qi = pl.program_id(1)
    if dil > 1:
        kd_scr, vd_scr, stage_scr, ostage_scr, lstage_scr = scratch

        @pl.when(qi == 0)
        def _():
            for src, dst in ((k_ref, kd_scr), (v_ref, vd_scr)):
                for lc in range(nlc):
                    stage_scr[lc] = src[:, lc * LANES:(lc + 1) * LANES].astype(F32)
                for r in range(dil):
                    for lc in range(nlc):
                        dst[r * seg:(r + 1) * seg, lc * LANES:(lc + 1) * LANES] = (
                            stage_scr[lc, pl.ds(r, seg, stride=dil), :].astype(BF16))

        for lc in range(nlc):
            stage_scr[lc, 0:nqb * tq * dil, :] = q_ref[:, lc * LANES:(lc + 1) * LANES].astype(F32)
        k_src, v_src = kd_scr, vd_scr
    else:
        k_src, v_src = k_ref, v_ref

    def block(r, u):
        a0 = (qi * nqb + u) * tq
        w0 = jnp.clip(a0 - DIL_SIDE, 0, seg - win)
        aq = a0 + lax.broadcasted_iota(I32, (DIL_HEADS * tq, win), 0) % tq
        ak = w0 + lax.broadcasted_iota(I32, (DIL_HEADS * tq, win), 1)
        valid = jnp.abs(aq - ak) <= DIL_SIDE
        row0 = u * tq * dil
        if dil > 1:
            q = jnp.concatenate([stage_scr[lc, pl.ds(row0 + r, tq, stride=dil), :] for lc in range(nlc)],
                                axis=1).astype(BF16)
        else:
            q = q_ref[row0:row0 + tq, :]
        start = pl.multiple_of(r * seg + w0, DIL_SIDE)
        kw = k_src[pl.ds(start, win), :]
        vw = v_src[pl.ds(start, win), :]
        q4, masks = _stack_heads(q, DIL_HEADS)
        s = jnp.where(valid, _nt_dot(q4, kw), NEG_INF)
        mx = jnp.max(s, axis=-1, keepdims=True)
        p = jnp.exp(s - mx)
        l = jnp.sum(p, axis=-1, keepdims=True)
        o = _unstack_heads(jnp.dot(p.astype(BF16), vw, preferred_element_type=F32) / l, masks)
        lse = _unstack_heads(jnp.broadcast_to(mx + jnp.log(l), (DIL_HEADS * tq, width)), masks)
        if dil > 1:
            for lc in range(nlc):
                ostage_scr[lc, pl.ds(row0 + r, tq, stride=dil), :] = o[:, lc * LANES:(lc + 1) * LANES]
                lstage_scr[lc, pl.ds(row0 + r, tq, stride=dil), :] = lse[:, lc * LANES:(lc + 1) * LANES]
        else:
            o_ref[row0:row0 + tq, :] = o
            lse_ref[row0:row0 + tq, :] = lse

    def residue(r):
        for u in range(nqb):
            block(r, u)

    if dil == 1:
        residue(0)
    else:
        def body(r, carry):
            residue(r)
            return carry

        lax.fori_loop(0, dil, body, 0, unroll=max(2, 4 // nqb))
        for lc in range(nlc):
            o_ref[:, lc * LANES:(lc + 1) * LANES] = ostage_scr[lc]
            lse_ref[:, lc * LANES:(lc + 1) * LANES] = lstage_scr[lc]


def _dil_attn(z, g, bsz, seq):
    dil = DIL_DILATIONS[g]
    n = z.shape[0]
    width = DIL_HEADS * HEAD_DIM
    seg = seq // dil
    tq = min(128, seg)
    win = min(2 * DIL_SIDE + tq, seg)
    nqb = max(1, min(4, seg // tq, DIL_MAX_STEP_ROWS // (tq * dil)))
    nq = seg // (tq * nqb)
    rows = nqb * tq * dil
    cq, ck, cv = (C_QD // width + g, C_KD // width + g, C_VD // width + g)
    scratch = []
    if dil > 1:
        nlc = width // LANES
        scratch = [pltpu.VMEM((seq, width), BF16), pltpu.VMEM((seq, width), BF16),
                   pltpu.VMEM((nlc, seq, LANES), F32),
                   pltpu.VMEM((nlc, rows, LANES), F32), pltpu.VMEM((nlc, rows, LANES), F32)]
    return pl.pallas_call(
        functools.partial(_dil_kernel, dil=dil, seg=seg, win=win, tq=tq, nqb=nqb),
        grid=(bsz, nq),
        in_specs=[pl.BlockSpec((rows, width), lambda b, i: (b * nq + i, cq)),
                  pl.BlockSpec((seq, width), lambda b, i: (b, ck)),
                  pl.BlockSpec((seq, width), lambda b, i: (b, cv))],
        out_specs=[pl.BlockSpec((rows, width), lambda b, i: (b * nq + i, 0)),
                   pl.BlockSpec((rows, width), lambda b, i: (b * nq + i, 0))],
        out_shape=[jax.ShapeDtypeStruct((n, width), F32), jax.ShapeDtypeStruct((n, width), F32)],
        scratch_shapes=scratch,
        compiler_params=_params("arbitrary", "arbitrary"),
        name=f"dil_attn_g{g}",
    )(z, z, z)


def _mix_out_kernel(x_ref, oa_ref, on_ref, od0_ref, od1_ref, od2_ref, l0_ref, l1_ref, l2_ref,
                    ga_ref, gn_ref, gd_ref, wpa_ref, wpn_ref, wpd_ref, wout_ref,
                    gm_ref, fg_ref, fsc_ref, fsh_ref, wr_ref, br_ref,
                    xo_ref, h2_ref, lg_ref, *, sub):
    def sig(g):
        return _sigmoid(g.astype(F32))

    for r0 in range(0, x_ref.shape[0], sub):
        rows = slice(r0, r0 + sub)
        l0, l1, l2 = l0_ref[rows, :], l1_ref[rows, :], l2_ref[rows, :]
        mx = jnp.maximum(jnp.maximum(l0, l1), l2)
        e0, e1, e2 = jnp.exp(l0 - mx), jnp.exp(l1 - mx), jnp.exp(l2 - mx)
        inv = 1.0 / (e0 + e1 + e2)
        od = (e0 * inv) * od0_ref[rows, :] + (e1 * inv) * od1_ref[rows, :] + (e2 * inv) * od2_ref[rows, :]
        merged = (sig(ga_ref[rows, :]) * jnp.dot(oa_ref[rows, :], wpa_ref[...], preferred_element_type=F32)
                  + sig(gn_ref[rows, :]) * jnp.dot(on_ref[rows, :], wpn_ref[...], preferred_element_type=F32)
                  + sig(gd_ref[rows, :]) * jnp.dot(od.astype(BF16), wpd_ref[...], preferred_element_type=F32))
        y = jnp.dot(merged.astype(BF16), wout_ref[...], preferred_element_type=F32)
        xn = x_ref[rows, :] + gm_ref[...] * y
        xo_ref[rows, :] = xn
        ms = jnp.mean(xn * xn, axis=-1, keepdims=True)
        h2 = xn * lax.rsqrt(ms + NORM_EPS) * fg_ref[...]
        h2 = h2 * (1.0 + fsc_ref[...]) + fsh_ref[...]
        lg_ref[:, rows] = lax.dot_general(wr_ref[...], h2, (((1,), (1,)), ((), ())), preferred_element_type=F32,
                                          precision=HIGHEST) + br_ref[...]
        for c in range(h2.shape[1] // LANES):
            h2_ref[pl.ds(r0 * SUBLANES + c, sub, stride=SUBLANES), :] = h2[:, c * LANES:(c + 1) * LANES]


def _mix_out(x2, z, o_a, o_n, o_d, lse_d, wpa, wpn, wpd, wout, gate_m, fg, fsc, fsh, wr, br, seq, tm=512, sub=256):
    n, d = x2.shape
    assert d // LANES == SUBLANES and C_GATE % d == 0
    spb = seq // tm
    gcol = C_GATE // d
    wd = DIL_HEADS * HEAD_DIM
    row = lambda w: pl.BlockSpec((tm, w), lambda i: (i, 0))
    full = lambda a: pl.BlockSpec(a.shape, lambda i: (0,) * a.ndim)
    perb = pl.BlockSpec((None, 1, d), lambda i: (i // spb, 0, 0))
    return pl.pallas_call(
        functools.partial(_mix_out_kernel, sub=min(sub, tm)),
        grid=(n // tm,),
        in_specs=[row(d), row(o_a.shape[1]), row(o_n.shape[1]), row(wd), row(wd), row(wd), row(wd), row(wd), row(wd),
                  pl.BlockSpec((tm, d), lambda i: (i, gcol)),
                  pl.BlockSpec((tm, d), lambda i: (i, gcol + 1)),
                  pl.BlockSpec((tm, d), lambda i: (i, gcol + 2)),
                  full(wpa), full(wpn), full(wpd), full(wout),
                  perb, full(fg), perb, perb, full(wr), full(br)],
        out_specs=[row(d), pl.BlockSpec((tm * SUBLANES, LANES), lambda i: (i, 0)),
                   pl.BlockSpec((wr.shape[0], tm), lambda i: (0, i))],
        out_shape=[jax.ShapeDtypeStruct((n, d), F32),
                   jax.ShapeDtypeStruct((n * SUBLANES, LANES), F32),
                   jax.ShapeDtypeStruct((wr.shape[0], n), F32)],
        compiler_params=_params("arbitrary"),
        name="mix_out",
    )(x2, o_a, o_n, o_d[0], o_d[1], o_d[2], lse_d[0], lse_d[1], lse_d[2], z, z, z,
      wpa, wpn, wpd, wout, gate_m, fg, fsc, fsh, wr, br)


def _route_kernel(lg_ref, e_ref, w_ref, cnt_ref):
    i = pl.program_id(0)
    le = lg_ref[0:N_EXPERTS, :]
    gl = lg_ref[N_EXPERTS:N_EXPERTS + N_GROUPS, :]
    tb = le.shape[1]
    gmax = jnp.max(gl, axis=0, keepdims=True)
    grow = lax.broadcasted_iota(I32, gl.shape, 0)
    gidx = jnp.min(jnp.where(gl == gmax, grow, N_GROUPS), axis=0, keepdims=True)
    pg_top = 1.0 / jnp.sum(jnp.exp(gl - gmax), axis=0, keepdims=True)
    erow = lax.broadcasted_iota(I32, le.shape, 0)
    lm = jnp.where((erow // EXPERTS_PER_GROUP) == gidx, le, -jnp.inf)
    m1 = jnp.max(lm, axis=0, keepdims=True)
    i1 = jnp.min(jnp.where(lm == m1, erow, N_EXPERTS), axis=0, keepdims=True)
    lm2 = jnp.where(erow == i1, -jnp.inf, lm)
    m2 = jnp.max(lm2, axis=0, keepdims=True)
    i2 = jnp.min(jnp.where(lm2 == m2, erow, N_EXPERTS), axis=0, keepdims=True)
    t = jnp.exp(m2 - m1)
    e_ref[0:1, :] = i1
    e_ref[1:2, :] = i2
    w_ref[0:1, :] = pg_top / (1.0 + t)
    w_ref[1:2, :] = pg_top * t / (1.0 + t)
    oh = (erow == i1).astype(F32) + (erow == i2).astype(F32)
    cnt = jnp.sum(oh, axis=1, keepdims=True)

    @pl.when(i == 0)
    def _():
        cnt_ref[...] = jnp.zeros_like(cnt_ref)

    cnt_ref[...] += jnp.broadcast_to(cnt, cnt_ref.shape)


def _route(lgt, tb=1024):
    rows, n = lgt.shape
    tb = min(tb, n)
    return pl.pallas_call(
        _route_kernel,
        grid=(n // tb,),
        in_specs=[pl.BlockSpec((rows, tb), lambda i: (0, i))],
        out_specs=[pl.BlockSpec((2, tb), lambda i: (0, i)),
                   pl.BlockSpec((2, tb), lambda i: (0, i)),
                   pl.BlockSpec((N_EXPERTS, LANES), lambda i: (0, 0))],
        out_shape=[jax.ShapeDtypeStruct((2, n), I32),
                   jax.ShapeDtypeStruct((2, n), F32),
                   jax.ShapeDtypeStruct((N_EXPERTS, LANES), F32)],
        compiler_params=_params("arbitrary"),
        name="route",
    )(lgt)


def _rank_kernel(e_ref, cnt_ref, dest_ref, be_ref, nu_ref, zs_ref, carry_scr, base_scr, *, blk):
    i = pl.program_id(0)

    @pl.when(i == 0)
    def _():
        cnt = cnt_ref[...]
        padded = jnp.floor((cnt + (blk - 1)) / blk) * blk
        r = lax.broadcasted_iota(I32, cnt.shape, 0)
        c = lax.broadcasted_iota(I32, cnt.shape, 1)
        prow = jnp.sum(jnp.where(r == c, padded, 0.0), axis=0, keepdims=True)
        pad_end = jnp.sum(jnp.where(c <= r, prow, 0.0), axis=1, keepdims=True)
        base = jnp.broadcast_to(pad_end, cnt.shape) - padded
        base_scr[...] = base
        carry_scr[...] = jnp.zeros_like(carry_scr)
        zs_ref[...] = jnp.sum(jnp.where(r == c, base + cnt, 0.0) + jnp.where(r + N_EXPERTS == c, padded - cnt, 0.0),
                              axis=0, keepdims=True).astype(I32)
        jb = lax.broadcasted_iota(I32, (N_EXPERTS, be_ref.shape[1]), 1).astype(F32) * blk
        be = jnp.sum((pad_end <= jb).astype(I32), axis=0, keepdims=True)
        be_ref[...] = jnp.minimum(be, N_EXPERTS - 1)
        nu_ref[...] = jnp.broadcast_to(jnp.max(pad_end, axis=0, keepdims=True) / blk, nu_ref.shape).astype(I32)

    e = e_ref[...]
    tb = e.shape[1]
    erow = lax.broadcasted_iota(I32, (N_EXPERTS, tb), 0)
    oh0 = erow == e[0:1]
    oh1 = erow == e[1:2]
    both = jnp.where(oh0 | oh1, 1.0, 0.0)
    upper = jnp.where(lax.broadcasted_iota(I32, (tb, tb), 0) < lax.broadcasted_iota(I32, (tb, tb), 1), 1.0, 0.0)
    prefix = jnp.dot(both.astype(BF16), upper.astype(BF16), preferred_element_type=F32)
    tot = prefix + base_scr[:, 0:1] + carry_scr[:, 0:1]
    dest_ref[0:1, :] = jnp.sum(jnp.where(oh0, tot, 0.0), axis=0, keepdims=True).astype(I32)
    dest_ref[1:2, :] = jnp.sum(jnp.where(oh1, tot, 0.0), axis=0, keepdims=True).astype(I32)
    carry_scr[...] += jnp.broadcast_to(jnp.sum(both, axis=1, keepdims=True), carry_scr.shape)


def _rank(eidx, cnt, n_blocks, blk, tb=512):
    n = eidx.shape[1]
    tb = min(tb, n)
    nbp = -(-n_blocks // LANES) * LANES
    return pl.pallas_call(
        functools.partial(_rank_kernel, blk=blk),
        grid=(n // tb,),
        in_specs=[pl.BlockSpec((2, tb), lambda i: (0, i)),
                  pl.BlockSpec(cnt.shape, lambda i: (0, 0))],
        out_specs=[pl.BlockSpec((2, tb), lambda i: (0, i)),
                   pl.BlockSpec((1, nbp), lambda i: (0, 0)),
                   pl.BlockSpec((1, LANES), lambda i: (0, 0)),
                   pl.BlockSpec((1, LANES), lambda i: (0, 0))],
        out_shape=[jax.ShapeDtypeStruct((2, n), I32),
                   jax.ShapeDtypeStruct((1, nbp), I32),
                   jax.ShapeDtypeStruct((1, LANES), I32),
                   jax.ShapeDtypeStruct((1, LANES), I32)],
        scratch_shapes=[pltpu.VMEM((N_EXPERTS, LANES), F32), pltpu.VMEM((N_EXPERTS, LANES), F32)],
        compiler_params=_params("arbitrary"),
        name="rank",
    )(eidx, cnt)


def _tile_copy(src, src_row, dst, dst_row, sem):
    return pltpu.make_async_copy(src.at[pl.ds(pl.multiple_of(src_row * SUBLANES, SUBLANES), SUBLANES), :],
                                 dst.at[pl.ds(pl.multiple_of(dst_row * SUBLANES, SUBLANES), SUBLANES), :], sem)


def _dispatch_kernel(zs_ref, nu_ref, dest_ref, h_ref, xs_hbm, zero_scr, sem, zsem, *, tb, blk, total_blocks):
    def zero_fill(slot, nslots):
        start = pl.multiple_of(slot * SUBLANES, SUBLANES)
        return pltpu.make_async_copy(zero_scr.at[pl.ds(0, nslots * SUBLANES), :],
                                     xs_hbm.at[pl.ds(start, nslots * SUBLANES), :], zsem)

    def pad_fills(act):
        for e in range(N_EXPERTS):
            off = zs_ref[e]
            count = zs_ref[N_EXPERTS + e]
            for b in range(blk.bit_length() - 1):
                bit = (count >> b) & 1

                @pl.when(bit == 1)
                def _(off=off, b=b):
                    act(zero_fill(off, 1 << b))

                off = off + bit * (1 << b)

    @pl.when(pl.program_id(0) == 0)
    def _():
        zero_scr[...] = jnp.zeros_like(zero_scr)
        pad_fills(lambda cp: cp.start())

        def tail_start(j, carry):
            zero_fill(j * blk, blk).start()
            return carry

        def tail_wait(j, carry):
            zero_fill(j * blk, blk).wait()
            return carry

        lax.fori_loop(nu_ref[0], total_blocks, tail_start, 0)
        pad_fills(lambda cp: cp.wait())
        lax.fori_loop(nu_ref[0], total_blocks, tail_wait, 0)

    def issue(t, carry):
        for k in range(2):
            _tile_copy(h_ref, t, xs_hbm, dest_ref[k, t], sem).start()
        return carry

    lax.fori_loop(0, tb, issue, 0)
    for k in range(2):
        pltpu.make_async_copy(h_ref, xs_hbm.at[pl.ds(0, tb * SUBLANES), :], sem).wait()


def _dispatch(zstart, n_used, dest, h2t, n_blocks, blk, tb=512):
    n = dest.shape[1]
    tb = min(tb, n)
    grid_spec = pltpu.PrefetchScalarGridSpec(
        num_scalar_prefetch=2,
        grid=(n // tb,),
        in_specs=[pl.BlockSpec((2, tb), lambda i, zs, nu: (0, i), memory_space=pltpu.SMEM),
                  pl.BlockSpec((tb * SUBLANES, LANES), lambda i, zs, nu: (i, 0))],
        out_specs=pl.BlockSpec(memory_space=pl.ANY),
        scratch_shapes=[pltpu.VMEM((blk * SUBLANES, LANES), F32), pltpu.SemaphoreType.DMA, pltpu.SemaphoreType.DMA],
    )
    total_blocks = n_blocks
    return pl.pallas_call(
        functools.partial(_dispatch_kernel, tb=tb, blk=blk, total_blocks=total_blocks),
        grid_spec=grid_spec,
        out_shape=jax.ShapeDtypeStruct((total_blocks * blk * SUBLANES, LANES), F32),
        compiler_params=pltpu.CompilerParams(dimension_semantics=("arbitrary",), has_side_effects=True),
        name="dispatch",
    )(zstart, n_used, dest, h2t)


def _expert_kernel(be_ref, first_ref, nu_ref, xs_ref, w1_ref, w3_ref, w2_ref, y_ref, xb_scr, w1_scr, w3_scr, w2_scr, *, blk):
    del be_ref
    j = pl.program_id(0)
    nchunk = xb_scr.shape[1] // LANES

    @pl.when(j < nu_ref[0])
    def _():
        @pl.when(first_ref[j] == 1)
        def _():
            w1_scr[...] = w1_ref[...].astype(BF16)
            w3_scr[...] = w3_ref[...].astype(BF16)
            w2_scr[...] = w2_ref[...].astype(BF16)

        for c in range(nchunk):
            xb_scr[:, c * LANES:(c + 1) * LANES] = xs_ref[pl.ds(c, blk, stride=SUBLANES), :].astype(BF16)
        xb = xb_scr[...]
        a = jnp.dot(xb, w1_scr[...], preferred_element_type=F32)
        b = jnp.dot(xb, w3_scr[...], preferred_element_type=F32)
        hid = (a * _sigmoid(a)) * b
        y = jnp.dot(hid.astype(BF16), w2_scr[...], preferred_element_type=F32)
        for c in range(nchunk):
            y_ref[pl.ds(c, blk, stride=SUBLANES), :] = y[:, c * LANES:(c + 1) * LANES]

    @pl.when(j >= nu_ref[0])
    def _():
        y_ref[...] = jnp.zeros_like(y_ref)


def _experts(block_e, first, n_used, xs, w1, w3, w2, layer, n_blocks, blk):
    d, hid = w1.shape[2], w1.shape[3]
    used = lambda j, nu: jnp.minimum(j, nu[0] - 1)
    grid_spec = pltpu.PrefetchScalarGridSpec(
        num_scalar_prefetch=3,
        grid=(n_blocks,),
        in_specs=[pl.BlockSpec((blk * SUBLANES, LANES), lambda j, be, fi, nu: (used(j, nu), 0)),
                  pl.BlockSpec((None, None, d, hid), lambda j, be, fi, nu: (layer, be[j], 0, 0)),
                  pl.BlockSpec((None, None, d, hid), lambda j, be, fi, nu: (layer, be[j], 0, 0)),
                  pl.BlockSpec((None, None, hid, d), lambda j, be, fi, nu: (layer, be[j], 0, 0))],
        out_specs=pl.BlockSpec((blk * SUBLANES, LANES), lambda j, be, fi, nu: (j, 0)),
        scratch_shapes=[pltpu.VMEM((blk, d), BF16), pltpu.VMEM((d, hid), BF16), pltpu.VMEM((d, hid), BF16),
                        pltpu.VMEM((hid, d), BF16)],
    )
    return pl.pallas_call(
        functools.partial(_expert_kernel, blk=blk),
        grid_spec=grid_spec,
        out_shape=jax.ShapeDtypeStruct((n_blocks * blk * SUBLANES, LANES), F32),
        compiler_params=_params("arbitrary"),
        name="experts",
    )(block_e, first, n_used, xs, w1, w3, w2)


def _combine_kernel(dcur_ref, dnext_ref, x_ref, w_ref, gf_ref, fg_ref, y_hbm, xo_ref, g_scr, sems, *, tb, nsteps):
    i = pl.program_id(0)
    slot = i % 2

    @pl.when(i == 0)
    def _():
        def issue(t, carry):
            for k in range(2):
                _tile_copy(y_hbm, dcur_ref[k, t], g_scr.at[0, k], t, sems.at[0]).start()
            return carry

        lax.fori_loop(0, tb, issue, 0)

    @pl.when(i + 1 < nsteps)
    def _():
        for t in range(tb):
            for k in range(2):
                _tile_copy(y_hbm, dnext_ref[k, t], g_scr.at[1 - slot, k], t, sems.at[1 - slot]).start()

    for k in range(2):
        pltpu.make_async_copy(y_hbm.at[pl.ds(0, tb * SUBLANES), :], g_scr.at[slot, k], sems.at[slot]).wait()
    w0 = w_ref[:, 0:1]
    w1 = w_ref[:, 1:2]
    for c in range(x_ref.shape[1] // LANES):
        sl = slice(c * LANES, (c + 1) * LANES)
        yc = (w0 * g_scr[slot, 0, pl.ds(c, tb, stride=SUBLANES), :]
              + w1 * g_scr[slot, 1, pl.ds(c, tb, stride=SUBLANES), :])
        xo_ref[:, sl] = x_ref[:, sl] + gf_ref[:, sl] * yc
    xn = xo_ref[...]
    ms = jnp.mean(xn * xn, axis=-1, keepdims=True)
    xo_ref[...] = xn * lax.rsqrt(ms + NORM_EPS) * fg_ref[...]


def _combine_final(dest, x2, wts_t, gate_f, y, final_g, seq, tb=256):
    n, d = x2.shape
    spb = seq // tb
    nsteps = n // tb
    return pl.pallas_call(
        functools.partial(_combine_kernel, tb=tb, nsteps=nsteps),
        grid=(nsteps,),
        in_specs=[pl.BlockSpec((2, tb), lambda i: (0, i), memory_space=pltpu.SMEM),
                  pl.BlockSpec((2, tb), lambda i: (0, jnp.minimum(i + 1, nsteps - 1)), memory_space=pltpu.SMEM),
                  pl.BlockSpec((tb, d), lambda i: (i, 0)),
                  pl.BlockSpec((tb, 2), lambda i: (i, 0)),
                  pl.BlockSpec((None, 1, d), lambda i: (i // spb, 0, 0)),
                  pl.BlockSpec((1, d), lambda i: (0, 0)),
                  pl.BlockSpec(memory_space=pl.ANY)],
        out_specs=pl.BlockSpec((tb, d), lambda i: (i, 0)),
        out_shape=jax.ShapeDtypeStruct((n, d), F32),
        scratch_shapes=[pltpu.VMEM((2, 2, tb * SUBLANES, LANES), F32), pltpu.SemaphoreType.DMA((2,))],
        compiler_params=_params("arbitrary"),
        name="combine_final",
    )(dest, dest, x2, wts_t, gate_f, final_g, y)


def _rotary_tables(seq, width):
    inv = 1.0 / (ROPE_THETA ** (jnp.arange(0, HEAD_DIM, 2, dtype=F32) / HEAD_DIM))
    ang = jnp.arange(seq, dtype=F32)[:, None] * inv[None, :]
    ang = jnp.concatenate([ang, ang], axis=-1)
    sign = jnp.concatenate([-jnp.ones((HEAD_DIM // 2,), F32), jnp.ones((HEAD_DIM // 2,), F32)])
    reps = width // HEAD_DIM
    return jnp.tile(jnp.cos(ang), (1, reps)), jnp.tile(jnp.sin(ang) * sign[None, :], (1, reps))


def _col_scale():
    colscale = jnp.ones((1, NC), F32)
    for lo, hi in ((C_QA, C_KA), (C_QD, C_KD), (C_QN, C_KN)):
        colscale = colscale.at[:, lo:hi].set(HEAD_DIM ** -0.5)
    return colscale.at[:, C_QA:C_KA].multiply(math.log2(math.e))


def kernel(x, c, ada_w, ada_b, mix_norm_g, ffn_norm_g, w_in, da_lambda, da_subln_g, na_rpb, w_proj_a, w_proj_n, w_proj_d, w_out, router_group_w, router_group_b, router_expert_w, router_expert_b, expert_w1, expert_w3, expert_w2, final_norm_g):
    bsz, seq, d = x.shape
    depth = ada_w.shape[0]
    n = bsz * seq
    tn = 512
    assert w_in.shape[2] == NC

    mods = _ada(c, ada_w, ada_b).reshape(depth, bsz, 6, 1, d)
    cos_t, sin_t = _rotary_tables(seq, LANES)
    colscale = _col_scale()
    w_bf = jnp.concatenate([w_in[:, :, lo:hi] for lo, hi in IN_PERM], axis=2).astype(BF16)

    n_blocks = (2 * n) // MOE_BLK + N_EXPERTS
    rpad = SUBLANES - N_GROUPS

    x2 = x.reshape(n, d)
    moe = None
    for l in range(depth):
        lam_init = 0.8 - 0.6 * math.exp(-0.3 * l)
        shift_m, scale_m, gate_m, shift_f, scale_f, gate_f = (mods[l, :, k] for k in range(6))
        proj_args = (mix_norm_g[l].reshape(1, d), scale_m, shift_m, cos_t, sin_t, colscale, w_bf, l, seq)
        if moe is None:
            z = _inproj(x2, *proj_args, tn=tn)
        else:
            x2, z = _combine_inproj(moe[0], x2, *moe[1:], *proj_args, tn=tn)
        o_a = _diff_attn(z, da_lambda[l], da_subln_g[l].reshape(1, 2 * HEAD_DIM), lam_init, bsz, seq)
        o_n = _na_attn(z, _na_bias_table(na_rpb[l]), bsz, seq)
        dil = [_dil_attn(z, g, bsz, seq) for g in range(len(DIL_DILATIONS))]
        wr = jnp.concatenate([router_expert_w[l], router_group_w[l], jnp.zeros((d, rpad), F32)], axis=1).T
        br = jnp.concatenate([router_expert_b[l], router_group_b[l], jnp.zeros((rpad,), F32)]).reshape(-1, 1)
        x2, h2t, lgt = _mix_out(
            x2, z, o_a, o_n, [o for o, _ in dil], [s for _, s in dil],
            w_proj_a[l].astype(BF16), w_proj_n[l].astype(BF16), w_proj_d[l].astype(BF16), w_out[l].astype(BF16),
            gate_m, ffn_norm_g[l].reshape(1, d), scale_f, shift_f, wr, br, seq)
        eidx, wts, cnt = _route(lgt)
        dest, block_e, n_used, zstart = _rank(eidx, cnt, n_blocks, MOE_BLK)
        block_e = block_e[0, :n_blocks]
        first = jnp.concatenate([jnp.ones((1,), I32), (block_e[1:] != block_e[:-1]).astype(I32)])
        n_used = n_used[0, :1]
        xs = _dispatch(zstart[0, :2 * N_EXPERTS], n_used, dest, h2t, n_blocks, MOE_BLK)
        y = _experts(block_e, first, n_used, xs, expert_w1, expert_w3, expert_w2, l, n_blocks, MOE_BLK)
        moe = (dest, wts.T, gate_f, y)
    return _combine_final(moe[0], x2, *moe[1:], final_norm_g.reshape(1, d), seq).reshape(bsz, seq, d)
```
